```python
import jax, jax.numpy as jnp
from jax import lax
import numpy as np

D_MODEL = 1024
BATCH = 8
SEQ = 4096
DEPTH = 1

PLE_DIM = 256
ROPE_THETA = 10000.0
RMS_EPS = 1e-6
BLOCK = 128

SWA_WINDOW = 128
A_HEADS = 8
A_KV_HEADS = 2
A_HEAD_DIM = 64
A_WIDTH = A_HEADS * A_HEAD_DIM

B_HEADS = 8
Q_LORA = 256
KV_LORA = 128
NOPE_DIM = 64
ROPE_DIM = 32
V_DIM = 64
B_WIDTH = B_HEADS * V_DIM

SPLIT_SIZES = (A_HEADS * A_HEAD_DIM,
               A_KV_HEADS * A_HEAD_DIM,
               A_KV_HEADS * A_HEAD_DIM,
               Q_LORA,
               KV_LORA,
               ROPE_DIM,
               2 * D_MODEL)
IN_COLS = 768 + 416 + 2 * D_MODEL

D_FF = 2816
CONV_W = 3

kernel_name = "hybrid_swa_mla_gated_block"


def split_cols(t, sizes):
    idx = np.cumsum(np.array(sizes))[:-1].tolist()
    return jnp.split(t, idx, axis=-1)


def rmsnorm(t, g):
    tf = t.astype(jnp.float32)
    y = tf * lax.rsqrt(jnp.mean(tf * tf, axis=-1, keepdims=True) + RMS_EPS)
    return (y * g.astype(jnp.float32)).astype(t.dtype)


def rope_tables(positions, dim):
    inv = ROPE_THETA ** (-(jnp.arange(0, dim, 2, dtype=jnp.float32) / dim))
    ang = positions.astype(jnp.float32)[..., None] * inv
    return jnp.cos(ang), jnp.sin(ang)


def apply_rope(t, cos, sin):
    tf = t.astype(jnp.float32)
    t1, t2 = jnp.split(tf, 2, axis=-1)
    c, s = cos[:, :, None, :], sin[:, :, None, :]
    return jnp.concatenate([t1 * c - t2 * s, t2 * c + t1 * s], axis=-1).astype(t.dtype)


def swa_attention(q, k, v, sinks):
    B_, S_, H, d = q.shape
    G = H // A_KV_HEADS
    nblk = S_ // BLOCK
    qb = q.reshape(B_, nblk, BLOCK, A_KV_HEADS, G, d)
    pad = ((0, 0), (BLOCK, 0), (0, 0), (0, 0))
    kp = jnp.pad(k, pad).reshape(B_, nblk + 1, BLOCK, A_KV_HEADS, d)
    vp = jnp.pad(v, pad).reshape(B_, nblk + 1, BLOCK, A_KV_HEADS, d)
    kb = jnp.concatenate([kp[:, :-1], kp[:, 1:]], axis=2)
    vb = jnp.concatenate([vp[:, :-1], vp[:, 1:]], axis=2)
    s = jnp.einsum('bnqkgd,bnskd->bnkgqs', qb, kb).astype(jnp.float32) * (d ** -0.5)
    qi = jnp.arange(BLOCK)[:, None]
    kj = jnp.arange(2 * BLOCK)[None, :]
    rel = qi + BLOCK - kj
    band = (rel >= 0) & (rel < SWA_WINDOW)
    key_abs = jnp.arange(nblk)[:, None, None] * BLOCK + kj[None] - BLOCK
    valid = band[None] & (key_abs >= 0)
    s = jnp.where(valid[None, :, None, None], s, -jnp.inf)
    sink = sinks.astype(jnp.float32).reshape(1, 1, A_KV_HEADS, G, 1, 1)
    m = jnp.maximum(jnp.max(s, axis=-1, keepdims=True), sink)
    e = jnp.exp(s - m)
    pr = e / (jnp.sum(e, axis=-1, keepdims=True) + jnp.exp(sink - m))
    out = jnp.einsum('bnkgqs,bnskd->bnqkgd', pr.astype(v.dtype), vb)
    return out.reshape(B_, S_, H * d)


def mla_attention(q, k, v):
    B_, S_, H, dqk = q.shape
    dv = v.shape[-1]
    nblk = S_ // BLOCK
    scale = dqk ** -0.5
    qb = q.reshape(B_, nblk, BLOCK, H, dqk).transpose(1, 0, 2, 3, 4)
    key_pos = jnp.arange(S_)

    def one_block(args):
        qblk, n = args
        s = jnp.einsum('bqhd,bshd->bhqs', qblk, k).astype(jnp.float32) * scale
        q_pos = n * BLOCK + jnp.arange(BLOCK)
        causal = key_pos[None, :] <= q_pos[:, None]
        pr = jax.nn.softmax(jnp.where(causal, s, -jnp.inf), axis=-1)
        return jnp.einsum('bhqs,bshd->bqhd', pr.astype(v.dtype), v)

    out = lax.map(one_block, (qb, jnp.arange(nblk)))
    return out.transpose(1, 0, 2, 3, 4).reshape(B_, S_, H * dv)


def causal_dwconv(u, w, b):
    C = u.shape[-1]
    y = lax.conv_general_dilated(u, w[:, None, :].astype(u.dtype), window_strides=(1,),
                                 padding=[(CONV_W - 1, 0)],
                                 dimension_numbers=('NWC', 'WIO', 'NWC'),
                                 feature_group_count=C)
    return y + b


def setup_inputs(seed: int = 0) -> dict:
    key = jax.random.key(seed)
    ks = jax.random.split(key, 24)
    f32 = jnp.float32

    def w(k, shape, fan_in):
        return jax.random.normal(k, shape, f32) * (fan_in ** -0.5)

    def gain(k, dim):
        return 1.0 + 0.02 * jax.random.normal(k, (DEPTH, dim), f32)

    x = jax.random.normal(ks[0], (BATCH, SEQ, D_MODEL), f32)
    p = jax.random.normal(ks[1], (DEPTH, BATCH, SEQ, PLE_DIM), f32)
    start = jax.random.randint(ks[2], (BATCH, 1), 0, 1024, dtype=jnp.int32)
    positions = (start + jnp.arange(SEQ, dtype=jnp.int32)[None, :]).astype(jnp.int32)
    return {
        "x": x,
        "p": p,
        "positions": positions,
        "attn_pre_norm": gain(ks[3], D_MODEL),
        "attn_post_norm": gain(ks[4], D_MODEL),
        "w_in": w(ks[5], (DEPTH, D_MODEL, IN_COLS), D_MODEL),
        "b_gate": 0.01 * jax.random.normal(ks[6], (DEPTH, 2 * D_MODEL), f32),
        "sinks": 0.5 * jax.random.normal(ks[7], (DEPTH, A_HEADS), f32),
        "q_a_norm": gain(ks[8], Q_LORA),
        "w_uq": w(ks[9], (DEPTH, Q_LORA, B_HEADS * (NOPE_DIM + ROPE_DIM)), Q_LORA),
        "kv_a_norm": gain(ks[10], KV_LORA),
        "w_ukv": w(ks[11], (DEPTH, KV_LORA, B_HEADS * (NOPE_DIM + V_DIM)), KV_LORA),
        "w_branch_a": w(ks[12], (DEPTH, A_WIDTH, D_MODEL), A_WIDTH),
        "w_branch_b": w(ks[13], (DEPTH, B_WIDTH, D_MODEL), B_WIDTH),
        "w_out": w(ks[14], (DEPTH, D_MODEL, D_MODEL), D_MODEL),
        "mlp_pre_norm": gain(ks[15], D_MODEL),
        "mlp_post_norm": gain(ks[16], D_MODEL),
        "w_up": w(ks[17], (DEPTH, D_MODEL, 2 * D_FF), D_MODEL),
        "conv_w": w(ks[18], (DEPTH, CONV_W, 2 * D_FF), CONV_W),
        "conv_b": 0.01 * jax.random.normal(ks[19], (DEPTH, 2 * D_FF), f32),
        "w_down": w(ks[20], (DEPTH, D_FF, D_MODEL), D_FF),
        "ple_norm": gain(ks[21], D_MODEL),
        "w_ple_gate": w(ks[22], (DEPTH, D_MODEL, D_MODEL), D_MODEL),
        "w_ple": w(ks[23], (DEPTH, PLE_DIM, D_MODEL), PLE_DIM),
    }


def reference(x, p, positions, attn_pre_norm, attn_post_norm, w_in, b_gate, sinks,
              q_a_norm, w_uq, kv_a_norm, w_ukv, w_branch_a, w_branch_b, w_out,
              mlp_pre_norm, mlp_post_norm, w_up, conv_w, conv_b, w_down,
              ple_norm, w_ple_gate, w_ple):
    B_, S_, _ = x.shape
    cos_a, sin_a = rope_tables(positions, A_HEAD_DIM)
    cos_b, sin_b = rope_tables(positions, ROPE_DIM)
    for i in range(DEPTH):
        h = rmsnorm(x, attn_pre_norm[i])
        qa, ka, va, cq, ckv, kr, gates = split_cols(h @ w_in[i], SPLIT_SIZES)

        qa = apply_rope(qa.reshape(B_, S_, A_HEADS, A_HEAD_DIM), cos_a, sin_a)
        ka = apply_rope(ka.reshape(B_, S_, A_KV_HEADS, A_HEAD_DIM), cos_a, sin_a)
        va = va.reshape(B_, S_, A_KV_HEADS, A_HEAD_DIM)
        ya = swa_attention(qa, ka, va, sinks[i])

        qb = (rmsnorm(cq, q_a_norm[i]) @ w_uq[i]).reshape(B_, S_, B_HEADS, NOPE_DIM + ROPE_DIM)
        q_nope, q_pe = split_cols(qb, (NOPE_DIM, ROPE_DIM))
        q_pe = apply_rope(q_pe, cos_b, sin_b)
        kvb = (rmsnorm(ckv, kv_a_norm[i]) @ w_ukv[i]).reshape(B_, S_, B_HEADS, NOPE_DIM + V_DIM)
        k_nope, vb = split_cols(kvb, (NOPE_DIM, V_DIM))
        k_pe = apply_rope(kr[:, :, None, :], cos_b, sin_b)
        qb = jnp.concatenate([q_nope, q_pe], axis=-1)
        kb = jnp.concatenate([k_nope, jnp.broadcast_to(k_pe, (B_, S_, B_HEADS, ROPE_DIM))], axis=-1)
        yb = mla_attention(qb, kb, vb)

        gate_a, gate_b = jnp.split(jax.nn.sigmoid(gates + b_gate[i]), 2, axis=-1)
        mixed = gate_a * (ya @ w_branch_a[i]) + gate_b * (yb @ w_branch_b[i])
        x = x + rmsnorm(mixed @ w_out[i], attn_post_norm[i])

        h = rmsnorm(x, mlp_pre_norm[i])
        u = causal_dwconv(h @ w_up[i], conv_w[i], conv_b[i])
        u_gate, u_val = jnp.split(u, 2, axis=-1)
        ff = (jax.nn.gelu(u_gate, approximate=True) * u_val) @ w_down[i]
        x = x + rmsnorm(ff, mlp_post_norm[i])

        e = p[i] @ w_ple[i]
        x = x + jax.nn.sigmoid(rmsnorm(x, ple_norm[i]) @ w_ple_gate[i]) * e
    return x
```

```python
import functools
import math

import jax
import jax.numpy as jnp
from jax import lax
from jax.experimental import pallas as pl
from jax.experimental.pallas import tpu as pltpu

D_MODEL = 1024
PLE_DIM = 256
ROPE_THETA = 10000.0
RMS_EPS = 1e-6
BLOCK = 128
A_HEADS = 8
A_KV_HEADS = 2
A_GROUP = A_HEADS // A_KV_HEADS
A_HEAD_DIM = 64
A_WIDTH = A_HEADS * A_HEAD_DIM
A_KV_WIDTH = A_KV_HEADS * A_HEAD_DIM
B_HEADS = 8
Q_LORA = 256
KV_LORA = 128
NOPE_DIM = 64
ROPE_DIM = 32
V_DIM = 64
B_WIDTH = B_HEADS * V_DIM
B_QK_PAD = 128
D_FF = 2816
CONV_W = 3
N_FM = A_WIDTH + 2 * A_KV_WIDTH + Q_LORA + KV_LORA + ROPE_DIM
VMEM_LIMIT_BYTES = 58 * 1024 * 1024

BF16 = jnp.bfloat16
F32 = jnp.float32
_TN = (((0,), (0,)), ((), ()))
_NT = (((1,), (1,)), ((), ()))


def _dot(a, b):
    return jnp.dot(a, b, preferred_element_type=F32)


def _rms_rows(t, g):
    return t * lax.rsqrt(jnp.mean(t * t, axis=-1, keepdims=True) + RMS_EPS) * g


def _rms_cols(t, g):
    return t * lax.rsqrt(jnp.mean(t * t, axis=0, keepdims=True) + RMS_EPS) * g


def _sigmoid(t):
    return 1.0 / (1.0 + jnp.exp(-t))


def _rope_rows(t1, t2, c, s):
    return t1 * c - t2 * s, t2 * c + t1 * s


def _inproj_kernel(x_ref, pos_ref, gpre_ref, wfm_ref, wg_ref, bg_ref, qng_ref, kvng_ref,
                   wuq_ref, wukv_ref, inva_ref, invb_ref,
                   qa_ref, ka_ref, va_ref, qb_ref, kb_ref, vb_ref, gates_ref):
    x = x_ref[0]
    h = _rms_rows(x, gpre_ref[...]).astype(BF16)
    gates = _dot(h, wg_ref[...]) + bg_ref[...]
    gates_ref[0] = _sigmoid(gates).astype(BF16)

    fm = lax.dot_general(wfm_ref[...], h, _NT, preferred_element_type=F32)
    pos = pos_ref[0]
    ang_a = pos * inva_ref[...]
    ca, sa = jnp.cos(ang_a), jnp.sin(ang_a)
    ang_b = pos * invb_ref[...]
    cb, sb = jnp.cos(ang_b), jnp.sin(ang_b)

    half_a = A_HEAD_DIM // 2
    scale_a = A_HEAD_DIM ** -0.5
    for hd in range(A_HEADS):
        r = hd * A_HEAD_DIM
        o1, o2 = _rope_rows(fm[r:r + half_a], fm[r + half_a:r + A_HEAD_DIM], ca, sa)
        qa_ref[0, r:r + half_a, :] = (o1 * scale_a).astype(BF16)
        qa_ref[0, r + half_a:r + A_HEAD_DIM, :] = (o2 * scale_a).astype(BF16)
    for hd in range(A_KV_HEADS):
        r = hd * A_HEAD_DIM
        f = A_WIDTH + r
        o1, o2 = _rope_rows(fm[f:f + half_a], fm[f + half_a:f + A_HEAD_DIM], ca, sa)
        ka_ref[0, r:r + half_a, :] = o1.astype(BF16)
        ka_ref[0, r + half_a:r + A_HEAD_DIM, :] = o2.astype(BF16)
    f = A_WIDTH + A_KV_WIDTH
    va_ref[0] = fm[f:f + A_KV_WIDTH].astype(BF16)

    f += A_KV_WIDTH
    cqn = _rms_cols(fm[f:f + Q_LORA], qng_ref[...]).astype(BF16)
    f += Q_LORA
    ckvn = _rms_cols(fm[f:f + KV_LORA], kvng_ref[...]).astype(BF16)
    f += KV_LORA
    kr = fm[f:f + ROPE_DIM]

    half_b = ROPE_DIM // 2
    scale_b = (NOPE_DIM + ROPE_DIM) ** -0.5
    qb = _dot(wuq_ref[...], cqn) * scale_b
    zeros_pad = jnp.zeros((B_QK_PAD - NOPE_DIM - ROPE_DIM, qb.shape[1]), BF16)
    for hd in range(B_HEADS):
        r = hd * B_QK_PAD
        qb_ref[0, r:r + NOPE_DIM, :] = qb[r:r + NOPE_DIM].astype(BF16)
        p1 = r + NOPE_DIM
        o1, o2 = _rope_rows(qb[p1:p1 + half_b], qb[p1 + half_b:p1 + ROPE_DIM], cb, sb)
        qb_ref[0, p1:p1 + half_b, :] = o1.astype(BF16)
        qb_ref[0, p1 + half_b:p1 + ROPE_DIM, :] = o2.astype(BF16)
        qb_ref[0, p1 + ROPE_DIM:r + B_QK_PAD, :] = zeros_pad

    kv = _dot(wukv_ref[...], ckvn)
    k1, k2 = _rope_rows(kr[:half_b], kr[half_b:], cb, sb)
    k1 = k1.astype(BF16)
    k2 = k2.astype(BF16)
    for hd in range(B_HEADS):
        r = hd * B_QK_PAD
        kb_ref[0, r:r + NOPE_DIM, :] = kv[hd * NOPE_DIM:(hd + 1) * NOPE_DIM].astype(BF16)
        p1 = r + NOPE_DIM
        kb_ref[0, p1:p1 + half_b, :] = k1
        kb_ref[0, p1 + half_b:p1 + ROPE_DIM, :] = k2
        kb_ref[0, p1 + ROPE_DIM:r + B_QK_PAD, :] = zeros_pad
    vb_ref[0] = kv[B_HEADS * NOPE_DIM:].astype(BF16)


def _inproj(x, pos, gpre, wfm, wg, bg, qng, kvng, wuq, wukv, inva, invb, *, tm):
    b, s, d = x.shape
    const = lambda shape: pl.BlockSpec(shape, lambda i, j: (0,) * len(shape))
    fm_out = lambda rows: pl.BlockSpec((1, rows, tm), lambda i, j: (i, 0, j))
    out_shape = (
        jax.ShapeDtypeStruct((b, A_WIDTH, s), BF16),
        jax.ShapeDtypeStruct((b, A_KV_WIDTH, s), BF16),
        jax.ShapeDtypeStruct((b, A_KV_WIDTH, s), BF16),
        jax.ShapeDtypeStruct((b, B_HEADS * B_QK_PAD, s), BF16),
        jax.ShapeDtypeStruct((b, B_HEADS * B_QK_PAD, s), BF16),
        jax.ShapeDtypeStruct((b, B_WIDTH, s), BF16),
        jax.ShapeDtypeStruct((b, s, 2 * d), BF16),
    )
    return pl.pallas_call(
        _inproj_kernel,
        grid=(b, s // tm),
        in_specs=[
            pl.BlockSpec((1, tm, d), lambda i, j: (i, j, 0)),
            pl.BlockSpec((1, 1, tm), lambda i, j: (i, 0, j)),
            const(gpre.shape), const(wfm.shape), const(wg.shape), const(bg.shape),
            const(qng.shape), const(kvng.shape), const(wuq.shape), const(wukv.shape),
            const(inva.shape), const(invb.shape),
        ],
        out_specs=(
            fm_out(A_WIDTH), fm_out(A_KV_WIDTH), fm_out(A_KV_WIDTH),
            fm_out(B_HEADS * B_QK_PAD), fm_out(B_HEADS * B_QK_PAD), fm_out(B_WIDTH),
            pl.BlockSpec((1, tm, 2 * d), lambda i, j: (i, j, 0)),
        ),
        out_shape=out_shape,
        compiler_params=pltpu.CompilerParams(
            dimension_semantics=("arbitrary", "arbitrary"), vmem_limit_bytes=VMEM_LIMIT_BYTES),
        name="inproj",
    )(x, pos, gpre, wfm, wg, bg, qng, kvng, wuq, wukv, inva, invb)


def _swa_kernel(q_ref, k_ref, v_ref, sink_ref, o_ref):
    seq = q_ref.shape[2]
    win = 2 * BLOCK
    cols = A_GROUP * BLOCK

    def block(n, carry):
        qs = pl.multiple_of(n * BLOCK, BLOCK)
        ks = pl.multiple_of(jnp.maximum(n - 1, 0) * BLOCK, BLOCK)
        kj = lax.broadcasted_iota(jnp.int32, (win, cols), 0)
        qi = lax.broadcasted_iota(jnp.int32, (win, cols), 1) & (BLOCK - 1)
        rel = qi - kj + (qs - ks)
        valid = (rel >= 0) & (rel < BLOCK)
        for kvh in range(A_KV_HEADS):
            r = kvh * A_HEAD_DIM
            k_t = k_ref[0, r:r + A_HEAD_DIM, pl.ds(ks, win)]
            v_t = v_ref[0, r:r + A_HEAD_DIM, pl.ds(ks, win)]
            q_t = jnp.concatenate(
                [q_ref[0, (kvh * A_GROUP + g) * A_HEAD_DIM:(kvh * A_GROUP + g + 1) * A_HEAD_DIM,
                       pl.ds(qs, BLOCK)] for g in range(A_GROUP)], axis=1)
            s_t = lax.dot_general(k_t, q_t, _TN, preferred_element_type=F32)
            s_t = jnp.where(valid, s_t, -jnp.inf)
            sink = sink_ref[kvh:kvh + 1, :]
            m = jnp.maximum(jnp.max(s_t, axis=0, keepdims=True), sink)
            e = jnp.exp(s_t - m)
            denom = jnp.sum(e, axis=0, keepdims=True) + jnp.exp(sink - m)
            o_t = _dot(v_t, e.astype(BF16)) * (1.0 / denom)
            for j in range(A_GROUP // 2):
                pair = jnp.concatenate(
                    [o_t[:, (2 * j) * BLOCK:(2 * j + 1) * BLOCK],
                     o_t[:, (2 * j + 1) * BLOCK:(2 * j + 2) * BLOCK]], axis=0)
                c = (kvh * (A_GROUP // 2) + j) * 2 * A_HEAD_DIM
                o_ref[0, pl.ds(qs, BLOCK), c:c + 2 * A_HEAD_DIM] = pair.T.astype(BF16)
        return carry

    lax.fori_loop(0, seq // BLOCK, block, 0)


def _swa(qa_t, ka_t, va_t, sink_rows):
    b, _, s = qa_t.shape
    return pl.pallas_call(
        _swa_kernel,
        grid=(b,),
        in_specs=[
            pl.BlockSpec((1, A_WIDTH, s), lambda i: (i, 0, 0)),
            pl.BlockSpec((1, A_KV_WIDTH, s), lambda i: (i, 0, 0)),
            pl.BlockSpec((1, A_KV_WIDTH, s), lambda i: (i, 0, 0)),
            pl.BlockSpec(sink_rows.shape, lambda i: (0, 0)),
        ],
        out_specs=pl.BlockSpec((1, s, A_WIDTH), lambda i: (i, 0, 0)),
        out_shape=jax.ShapeDtypeStruct((b, s, A_WIDTH), BF16),
        compiler_params=pltpu.CompilerParams(
            dimension_semantics=("arbitrary",), vmem_limit_bytes=VMEM_LIMIT_BYTES),
        name="swa",
    )(qa_t, ka_t, va_t, sink_rows)


def _mla_kernel(q_ref, k_ref, v_ref, o_ref, *, tile):
    seq = q_ref.shape[2]
    heads = 2

    def q_tile(qt, carry):
        q0 = pl.multiple_of(qt * tile, tile)
        kj = lax.broadcasted_iota(jnp.int32, (tile, tile), 0)
        qi = lax.broadcasted_iota(jnp.int32, (tile, tile), 1)
        causal = kj <= qi

        def scores(hh, k0):
            k_t = k_ref[0, hh * B_QK_PAD:(hh + 1) * B_QK_PAD, pl.ds(k0, tile)]
            q_t = q_ref[0, hh * B_QK_PAD:(hh + 1) * B_QK_PAD, pl.ds(q0, tile)]
            return lax.dot_general(k_t, q_t, _TN, preferred_element_type=F32)

        def values(hh, k0):
            return v_ref[0, hh * V_DIM:(hh + 1) * V_DIM, pl.ds(k0, tile)]

        state = []
        for hh in range(heads):
            s_t = jnp.where(causal, scores(hh, q0), -jnp.inf)
            m = jnp.max(s_t, axis=0, keepdims=True)
            p = jnp.exp(s_t - m)
            l = jnp.sum(p, axis=0, keepdims=True)
            acc = _dot(values(hh, q0), p.astype(BF16))
            state += [m, l, acc]

        def k_tile(kt, st):
            k0 = pl.multiple_of(kt * tile, tile)
            new = []
            for hh in range(heads):
                m, l, acc = st[3 * hh:3 * hh + 3]
                s_t = scores(hh, k0)
                m_new = jnp.maximum(m, jnp.max(s_t, axis=0, keepdims=True))
                alpha = jnp.exp(m - m_new)
                p = jnp.exp(s_t - m_new)
                l = alpha * l + jnp.sum(p, axis=0, keepdims=True)
                acc = alpha * acc + _dot(values(hh, k0), p.astype(BF16))
                new += [m_new, l, acc]
            return tuple(new)

        st = lax.fori_loop(0, qt, k_tile, tuple(state))
        outs = [st[3 * hh + 2] * (1.0 / st[3 * hh + 1]) for hh in range(heads)]
        pair = jnp.concatenate(outs, axis=0)
        o_ref[0, pl.ds(q0, tile), :] = pair.T.astype(BF16)
        return carry

    lax.fori_loop(0, seq // tile, q_tile, 0)


def _mla(qb_t, kb_t, vb_t, *, tile):
    b, _, s = qb_t.shape
    pairs = B_HEADS // 2
    return pl.pallas_call(
        functools.partial(_mla_kernel, tile=tile),
        grid=(b, pairs),
        in_specs=[
            pl.BlockSpec((1, 2 * B_QK_PAD, s), lambda i, j: (i, j, 0)),
            pl.BlockSpec((1, 2 * B_QK_PAD, s), lambda i, j: (i, j, 0)),
            pl.BlockSpec((1, 2 * V_DIM, s), lambda i, j: (i, j, 0)),
        ],
        out_specs=pl.BlockSpec((1, s, 2 * V_DIM), lambda i, j: (i, 0, j)),
        out_shape=jax.ShapeDtypeStruct((b, s, B_WIDTH), BF16),
        compiler_params=pltpu.CompilerParams(
            dimension_semantics=("arbitrary", "arbitrary"), vmem_limit_bytes=VMEM_LIMIT_BYTES),
        name="mla",
    )(qb_t, kb_t, vb_t)


def _gelu_tanh(t):
    return 0.5 * t * (1.0 + jnp.tanh(math.sqrt(2.0 / math.pi) * (t + 0.044715 * (t * t * t))))


def _post_kernel(x_ref, ya_ref, yb_ref, g_ref, p_ref, wa_ref, wb_ref, wout_ref, gpost_ref,
                 gmlp_ref, wup_ref, cw_ref, cb_ref, wdown_ref, gmlppost_ref, gple_ref,
                 wpleg_ref, wple_ref, o_ref, ubuf_ref, *, tm):
    carry_rows = 8

    @pl.when(pl.program_id(1) == 0)
    def _():
        ubuf_ref[0:carry_rows, :] = jnp.zeros((carry_rows, 2 * D_FF), F32)

    x = x_ref[0]
    g = g_ref[0]
    ma = _dot(ya_ref[0], wa_ref[...])
    mb = _dot(yb_ref[0], wb_ref[...])
    mixed = g[:, :D_MODEL].astype(F32) * ma + g[:, D_MODEL:].astype(F32) * mb
    x1 = x + _rms_rows(_dot(mixed.astype(BF16), wout_ref[...]), gpost_ref[...])

    h2 = _rms_rows(x1, gmlp_ref[...]).astype(BF16)
    u = _dot(h2, wup_ref[...])
    ubuf_ref[carry_rows:carry_rows + tm, :] = u
    u1 = ubuf_ref[carry_rows - 1:carry_rows - 1 + tm, :]
    u2 = ubuf_ref[carry_rows - 2:carry_rows - 2 + tm, :]
    cw = cw_ref[...]
    y = cw[0:1] * u2 + cw[1:2] * u1 + cw[2:3] * u + cb_ref[...]
    ubuf_ref[0:carry_rows, :] = ubuf_ref[tm:tm + carry_rows, :]
    act = _gelu_tanh(y[:, :D_FF]) * y[:, D_FF:]
    ff = _dot(act.astype(BF16), wdown_ref[...])
    x2 = x1 + _rms_rows(ff, gmlppost_ref[...])

    e = _dot(p_ref[0].astype(BF16), wple_ref[...])
    gate = _sigmoid(_dot(_rms_rows(x2, gple_ref[...]).astype(BF16), wpleg_ref[...]))
    o_ref[0] = x2 + gate * e


def _post(x, ya, yb, gates, p, wa, wb, wout, gpost, gmlp, wup, cw, cb, wdown, gmlppost, gple,
          wpleg, wple, *, tm):
    b, s, d = x.shape
    const = lambda a: pl.BlockSpec(a.shape, lambda i, j: (0,) * a.ndim,
                                   pipeline_mode=pl.Buffered(1))
    rows = lambda n: pl.BlockSpec((1, tm, n), lambda i, j: (i, j, 0))
    weights = (wa, wb, wout, gpost, gmlp, wup, cw, cb, wdown, gmlppost, gple, wpleg, wple)
    return pl.pallas_call(
        functools.partial(_post_kernel, tm=tm),
        grid=(b, s // tm),
        in_specs=[rows(d), rows(A_WIDTH), rows(B_WIDTH), rows(2 * d), rows(PLE_DIM)]
        + [const(w) for w in weights],
        out_specs=rows(d),
        out_shape=jax.ShapeDtypeStruct((b, s, d), F32),
        scratch_shapes=[pltpu.VMEM((tm + 8, 2 * D_FF), F32)],
        compiler_params=pltpu.CompilerParams(
            dimension_semantics=("arbitrary", "arbitrary"), vmem_limit_bytes=VMEM_LIMIT_BYTES),
        name="post",
    )(x, ya, yb, gates, p, *weights)


def _layer(x, p_i, pos_f, inva, invb, attn_pre_norm, attn_post_norm, w_in, b_gate, sinks,
           q_a_norm, w_uq, kv_a_norm, w_ukv, w_branch_a, w_branch_b, w_out, mlp_pre_norm,
           mlp_post_norm, w_up, conv_w, conv_b, w_down, ple_norm, w_ple_gate, w_ple):
    row = lambda v: v.reshape(1, -1)
    col = lambda v: v.reshape(-1, 1)
    wfm = w_in[:, :N_FM].T.astype(BF16)
    wg = w_in[:, N_FM:].astype(BF16)
    wuq = jnp.pad(w_uq.reshape(Q_LORA, B_HEADS, NOPE_DIM + ROPE_DIM),
                  ((0, 0), (0, 0), (0, B_QK_PAD - NOPE_DIM - ROPE_DIM)))
    wuq = wuq.reshape(Q_LORA, B_HEADS * B_QK_PAD).T.astype(BF16)
    wukv = w_ukv.reshape(KV_LORA, B_HEADS, NOPE_DIM + V_DIM)
    wukv = jnp.concatenate([wukv[:, :, :NOPE_DIM].reshape(KV_LORA, -1),
                            wukv[:, :, NOPE_DIM:].reshape(KV_LORA, -1)], axis=1)
    wukv = wukv.T.astype(BF16)

    qa_t, ka_t, va_t, qb_t, kb_t, vb_t, gates = _inproj(
        x, pos_f, row(attn_pre_norm), wfm, wg, row(b_gate), col(q_a_norm), col(kv_a_norm),
        wuq, wukv, inva, invb, tm=512)

    sink_rows = jnp.repeat(sinks.astype(F32), BLOCK).reshape(A_KV_HEADS, A_GROUP * BLOCK)
    ya = _swa(qa_t, ka_t, va_t, sink_rows)
    yb = _mla(qb_t, kb_t, vb_t, tile=512)

    return _post(
        x, ya, yb, gates, p_i, w_branch_a.astype(BF16), w_branch_b.astype(BF16),
        w_out.astype(BF16), row(attn_post_norm), row(mlp_pre_norm), w_up.astype(BF16),
        conv_w, row(conv_b), w_down.astype(BF16), row(mlp_post_norm), row(ple_norm),
        w_ple_gate.astype(BF16), w_ple.astype(BF16), tm=256)


def kernel(x, p, positions, attn_pre_norm, attn_post_norm, w_in, b_gate, sinks, q_a_norm, w_uq,
           kv_a_norm, w_ukv, w_branch_a, w_branch_b, w_out, mlp_pre_norm, mlp_post_norm, w_up,
           conv_w, conv_b, w_down, ple_norm, w_ple_gate, w_ple):
    b, s, _ = x.shape
    pos_f = positions.astype(F32).reshape(b, 1, s)
    inva = (ROPE_THETA ** (-(jnp.arange(0, A_HEAD_DIM, 2, dtype=F32) / A_HEAD_DIM))).reshape(-1, 1)
    invb = (ROPE_THETA ** (-(jnp.arange(0, ROPE_DIM, 2, dtype=F32) / ROPE_DIM))).reshape(-1, 1)
    params = (attn_pre_norm, attn_post_norm, w_in, b_gate, sinks, q_a_norm, w_uq, kv_a_norm,
              w_ukv, w_branch_a, w_branch_b, w_out, mlp_pre_norm, mlp_post_norm, w_up, conv_w,
              conv_b, w_down, ple_norm, w_ple_gate, w_ple)
    for i in range(p.shape[0]):
        x = _layer(x, p[i], pos_f, inva, invb, *(w[i] for w in params))
    return x
```

```python
import functools
import math

import jax
import jax.numpy as jnp
from jax import lax
from jax.experimental import pallas as pl
from jax.experimental.pallas import tpu as pltpu

D_MODEL = 1024
PLE_DIM = 256
ROPE_THETA = 10000.0
RMS_EPS = 1e-6
BLOCK = 128
A_HEADS = 8
A_KV_HEADS = 2
A_GROUP = A_HEADS // A_KV_HEADS
A_HEAD_DIM = 64
A_WIDTH = A_HEADS * A_HEAD_DIM
A_KV_WIDTH = A_KV_HEADS * A_HEAD_DIM
B_HEADS = 8
Q_LORA = 256
KV_LORA = 128
NOPE_DIM = 64
ROPE_DIM = 32
V_DIM = 64
B_WIDTH = B_HEADS * V_DIM
B_QK_PAD = 128
ONES_ROWS = 16
D_FF = 2816
CONV_W = 3
N_FM = A_WIDTH + 2 * A_KV_WIDTH + Q_LORA + KV_LORA + ROPE_DIM
VMEM_LIMIT_BYTES = 58 * 1024 * 1024
LOG2_E = math.log2(math.e)

BF16 = jnp.bfloat16
F32 = jnp.float32
_TN = (((0,), (0,)), ((), ()))
_NT = (((1,), (1,)), ((), ()))


def _dot(a, b):
    return jnp.dot(a, b, preferred_element_type=F32)


def _rms_rows(t, g):
    return t * lax.rsqrt(jnp.mean(t * t, axis=-1, keepdims=True) + RMS_EPS) * g


def _rms_cols(t, g):
    return t * lax.rsqrt(jnp.mean(t * t, axis=0, keepdims=True) + RMS_EPS) * g


def _sigmoid(t):
    return 1.0 / (1.0 + jnp.exp(-t))


def _rope_rows(t1, t2, c, s):
    return t1 * c - t2 * s, t2 * c + t1 * s


def _inproj_kernel(x_ref, pos_ref, gpre_ref, wfm_ref, wg_ref, bg_ref, qng_ref, kvng_ref,
                   wuq_ref, wukv_ref, inva_ref, invb_ref,
                   qa_ref, ka_ref, va_ref, qb_ref, kb_ref, vb_ref, gates_ref):
    x = x_ref[0]
    h = _rms_rows(x, gpre_ref[...]).astype(BF16)
    gates = _dot(h, wg_ref[...]) + bg_ref[...]
    gates_ref[0] = _sigmoid(gates).astype(BF16)

    fm = lax.dot_general(wfm_ref[...], h, _NT, preferred_element_type=F32)
    pos = pos_ref[0]
    ang_a = pos * inva_ref[...]
    ca, sa = jnp.cos(ang_a), jnp.sin(ang_a)
    ang_b = pos * invb_ref[...]
    cb, sb = jnp.cos(ang_b), jnp.sin(ang_b)

    half_a = A_HEAD_DIM // 2
    scale_a = LOG2_E * A_HEAD_DIM ** -0.5
    for hd in range(A_HEADS):
        r = hd * A_HEAD_DIM
        o1, o2 = _rope_rows(fm[r:r + half_a], fm[r + half_a:r + A_HEAD_DIM], ca, sa)
        qa_ref[0, r:r + half_a, :] = (o1 * scale_a).astype(BF16)
        qa_ref[0, r + half_a:r + A_HEAD_DIM, :] = (o2 * scale_a).astype(BF16)
    for hd in range(A_KV_HEADS):
        r = hd * A_HEAD_DIM
        f = A_WIDTH + r
        o1, o2 = _rope_rows(fm[f:f + half_a], fm[f + half_a:f + A_HEAD_DIM], ca, sa)
        ka_ref[0, r:r + half_a, :] = o1.astype(BF16)
        ka_ref[0, r + half_a:r + A_HEAD_DIM, :] = o2.astype(BF16)
    f = A_WIDTH + A_KV_WIDTH
    va_ref[0] = fm[f:f + A_KV_WIDTH].astype(BF16)

    f += A_KV_WIDTH
    cqn = _rms_cols(fm[f:f + Q_LORA], qng_ref[...]).astype(BF16)
    f += Q_LORA
    ckvn = _rms_cols(fm[f:f + KV_LORA], kvng_ref[...]).astype(BF16)
    f += KV_LORA
    kr = fm[f:f + ROPE_DIM]

    half_b = ROPE_DIM // 2
    scale_b = LOG2_E * (NOPE_DIM + ROPE_DIM) ** -0.5
    qb = _dot(wuq_ref[...], cqn) * scale_b
    zeros_pad = jnp.zeros((B_QK_PAD - NOPE_DIM - ROPE_DIM, qb.shape[1]), BF16)
    for hd in range(B_HEADS):
        r = hd * B_QK_PAD
        qb_ref[0, r:r + NOPE_DIM, :] = qb[r:r + NOPE_DIM].astype(BF16)
        p1 = r + NOPE_DIM
        o1, o2 = _rope_rows(qb[p1:p1 + half_b], qb[p1 + half_b:p1 + ROPE_DIM], cb, sb)
        qb_ref[0, p1:p1 + half_b, :] = o1.astype(BF16)
        qb_ref[0, p1 + half_b:p1 + ROPE_DIM, :] = o2.astype(BF16)
        qb_ref[0, p1 + ROPE_DIM:r + B_QK_PAD, :] = zeros_pad

    kv = _dot(wukv_ref[...], ckvn)
    k1, k2 = _rope_rows(kr[:half_b], kr[half_b:], cb, sb)
    k1 = k1.astype(BF16)
    k2 = k2.astype(BF16)
    for hd in range(B_HEADS):
        r = hd * B_QK_PAD
        kb_ref[0, r:r + NOPE_DIM, :] = kv[hd * NOPE_DIM:(hd + 1) * NOPE_DIM].astype(BF16)
        p1 = r + NOPE_DIM
        kb_ref[0, p1:p1 + half_b, :] = k1
        kb_ref[0, p1 + half_b:p1 + ROPE_DIM, :] = k2
        kb_ref[0, p1 + ROPE_DIM:r + B_QK_PAD, :] = zeros_pad
    vb_ref[0] = kv[B_HEADS * NOPE_DIM:].astype(BF16)


def _inproj(x, pos, gpre, wfm, wg, bg, qng, kvng, wuq, wukv, inva, invb, *, tm):
    b, s, d = x.shape
    const = lambda shape: pl.BlockSpec(shape, lambda i, j: (0,) * len(shape))
    fm_out = lambda rows: pl.BlockSpec((1, rows, tm), lambda i, j: (i, 0, j))
    out_shape = (
        jax.ShapeDtypeStruct((b, A_WIDTH, s), BF16),
        jax.ShapeDtypeStruct((b, A_KV_WIDTH, s), BF16),
        jax.ShapeDtypeStruct((b, A_KV_WIDTH, s), BF16),
        jax.ShapeDtypeStruct((b, B_HEADS * B_QK_PAD, s), BF16),
        jax.ShapeDtypeStruct((b, B_HEADS * B_QK_PAD, s), BF16),
        jax.ShapeDtypeStruct((b, B_WIDTH, s), BF16),
        jax.ShapeDtypeStruct((b, s, 2 * d), BF16),
    )
    return pl.pallas_call(
        _inproj_kernel,
        grid=(b, s // tm),
        in_specs=[
            pl.BlockSpec((1, tm, d), lambda i, j: (i, j, 0)),
            pl.BlockSpec((1, 1, tm), lambda i, j: (i, 0, j)),
            const(gpre.shape), const(wfm.shape), const(wg.shape), const(bg.shape),
            const(qng.shape), const(kvng.shape), const(wuq.shape), const(wukv.shape),
            const(inva.shape), const(invb.shape),
        ],
        out_specs=(
            fm_out(A_WIDTH), fm_out(A_KV_WIDTH), fm_out(A_KV_WIDTH),
            fm_out(B_HEADS * B_QK_PAD), fm_out(B_HEADS * B_QK_PAD), fm_out(B_WIDTH),
            pl.BlockSpec((1, tm, 2 * d), lambda i, j: (i, j, 0)),
        ),
        out_shape=out_shape,
        compiler_params=pltpu.CompilerParams(
            dimension_semantics=("arbitrary", "arbitrary"), vmem_limit_bytes=VMEM_LIMIT_BYTES),
        name="inproj",
    )(x, pos, gpre, wfm, wg, bg, qng, kvng, wuq, wukv, inva, invb)


def _swa_kernel(q_ref, k_ref, v_ref, sink_ref, o_ref):
    seq = q_ref.shape[2]
    win = 2 * BLOCK
    cols = A_GROUP * BLOCK

    def block(n, carry):
        qs = pl.multiple_of(n * BLOCK, BLOCK)
        ks = pl.multiple_of(jnp.maximum(n - 1, 0) * BLOCK, BLOCK)
        kj = lax.broadcasted_iota(jnp.int32, (win, cols), 0)
        qi = lax.broadcasted_iota(jnp.int32, (win, cols), 1) & (BLOCK - 1)
        rel = qi - kj + (qs - ks)
        valid = (rel >= 0) & (rel < BLOCK)
        for kvh in range(A_KV_HEADS):
            r = kvh * A_HEAD_DIM
            k_t = k_ref[0, r:r + A_HEAD_DIM, pl.ds(ks, win)]
            v_t = v_ref[0, r:r + A_HEAD_DIM, pl.ds(ks, win)]
            q_t = jnp.concatenate(
                [q_ref[0, (kvh * A_GROUP + g) * A_HEAD_DIM:(kvh * A_GROUP + g + 1) * A_HEAD_DIM,
                       pl.ds(qs, BLOCK)] for g in range(A_GROUP)], axis=1)
            s_t = lax.dot_general(k_t, q_t, _TN, preferred_element_type=F32)
            s_t = jnp.where(valid, s_t, -jnp.inf)
            sink = sink_ref[kvh:kvh + 1, :] * LOG2_E
            m = jnp.maximum(jnp.max(s_t, axis=0, keepdims=True), sink)
            e = jnp.exp2(s_t - m)
            denom = jnp.sum(e, axis=0, keepdims=True) + jnp.exp2(sink - m)
            o_t = _dot(v_t, e.astype(BF16)) * (1.0 / denom)
            for j in range(A_GROUP // 2):
                pair = jnp.concatenate(
                    [o_t[:, (2 * j) * BLOCK:(2 * j + 1) * BLOCK],
                     o_t[:, (2 * j + 1) * BLOCK:(2 * j + 2) * BLOCK]], axis=0)
                c = (kvh * (A_GROUP // 2) + j) * 2 * A_HEAD_DIM
                o_ref[0, pl.ds(qs, BLOCK), c:c + 2 * A_HEAD_DIM] = pair.T.astype(BF16)
        return carry

    lax.fori_loop(0, seq // BLOCK, block, 0)


def _swa(qa_t, ka_t, va_t, sink_rows):
    b, _, s = qa_t.shape
    return pl.pallas_call(
        _swa_kernel,
        grid=(b,),
        in_specs=[
            pl.BlockSpec((1, A_WIDTH, s), lambda i: (i, 0, 0)),
            pl.BlockSpec((1, A_KV_WIDTH, s), lambda i: (i, 0, 0)),
            pl.BlockSpec((1, A_KV_WIDTH, s), lambda i: (i, 0, 0)),
            pl.BlockSpec(sink_rows.shape, lambda i: (0, 0)),
        ],
        out_specs=pl.BlockSpec((1, s, A_WIDTH), lambda i: (i, 0, 0)),
        out_shape=jax.ShapeDtypeStruct((b, s, A_WIDTH), BF16),
        compiler_params=pltpu.CompilerParams(
            dimension_semantics=("arbitrary",), vmem_limit_bytes=VMEM_LIMIT_BYTES),
        name="swa",
    )(qa_t, ka_t, va_t, sink_rows)


def _mla_kernel(q_ref, k_ref, v_ref, o_ref, sa_ref, sb_ref, m_ref, acc_ref, *, heads, tq, tk):
    seq = q_ref.shape[2]
    n_q = seq // tq
    ones = jnp.ones((ONES_ROWS, tk), BF16)

    def scores_to(buf, q0, k0, qoff, qw):
        for hh in range(heads):
            r = hh * B_QK_PAD
            k_t = k_ref[0, r:r + B_QK_PAD, pl.ds(k0, tk)]
            q_t = q_ref[0, r:r + B_QK_PAD, pl.ds(q0 + qoff, qw)]
            buf[hh, :, qoff:qoff + qw] = lax.dot_general(
                k_t, q_t, _TN, preferred_element_type=F32)

    def consume(buf, k0, qoff, qw, mask=None):
        for hh in range(heads):
            s_t = buf[hh, :, qoff:qoff + qw]
            if mask is not None:
                s_t = jnp.where(mask, s_t, -jnp.inf)
            m_prev = m_ref[hh, :, qoff:qoff + qw]
            m_new = jnp.maximum(m_prev, jnp.max(s_t, axis=0, keepdims=True))
            alpha = jnp.exp2(m_prev - m_new)
            p = jnp.exp2(s_t - m_new).astype(BF16)
            v_t = jnp.concatenate(
                [v_ref[0, hh * V_DIM:(hh + 1) * V_DIM, pl.ds(k0, tk)], ones], axis=0)
            acc_ref[hh, :, qoff:qoff + qw] = alpha * acc_ref[hh, :, qoff:qoff + qw] + _dot(v_t, p)
            m_ref[hh, :, qoff:qoff + qw] = m_new

    def causal(width):
        kj = lax.broadcasted_iota(jnp.int32, (tk, width), 0)
        qi = lax.broadcasted_iota(jnp.int32, (tk, width), 1)
        return kj <= qi

    def q_tile(qt, carry):
        q0 = pl.multiple_of(qt * tq, tq)
        m_ref[...] = jnp.full(m_ref.shape, -jnp.inf, F32)
        acc_ref[...] = jnp.zeros(acc_ref.shape, F32)

        def pair(j, c):
            k0 = pl.multiple_of(j * tq, tq)
            scores_to(sb_ref, q0, k0 + tk, 0, tq)
            consume(sa_ref, k0, 0, tq)
            scores_to(sa_ref, q0, k0 + 2 * tk, 0, tq)
            consume(sb_ref, k0 + tk, 0, tq)
            return c

        lax.fori_loop(0, qt, pair, 0)

        scores_to(sb_ref, q0, q0 + tk, tk, tq - tk)
        consume(sa_ref, q0, 0, tq, mask=causal(tq))
        q_next = pl.multiple_of(jnp.minimum(qt + 1, n_q - 1) * tq, tq)
        scores_to(sa_ref, q_next, 0, 0, tq)
        consume(sb_ref, q0 + tk, tk, tq - tk, mask=causal(tq - tk))

        outs = [acc_ref[hh, :V_DIM, :] * (1.0 / acc_ref[hh, V_DIM:V_DIM + 1, :])
                for hh in range(heads)]
        for hp in range(heads // 2):
            pair_t = jnp.concatenate(outs[2 * hp:2 * hp + 2], axis=0)
            o_ref[0, pl.ds(q0, tq), hp * 2 * V_DIM:(hp + 1) * 2 * V_DIM] = pair_t.T.astype(BF16)
        return carry

    scores_to(sa_ref, 0, 0, 0, tq)
    lax.fori_loop(0, n_q, q_tile, 0)


def _mla(qb_t, kb_t, vb_t, *, heads, tq, tk):
    assert tq == 2 * tk and heads % 2 == 0 and B_HEADS % heads == 0
    b, _, s = qb_t.shape
    return pl.pallas_call(
        functools.partial(_mla_kernel, heads=heads, tq=tq, tk=tk),
        grid=(b, B_HEADS // heads),
        in_specs=[
            pl.BlockSpec((1, heads * B_QK_PAD, s), lambda i, j: (i, j, 0)),
            pl.BlockSpec((1, heads * B_QK_PAD, s), lambda i, j: (i, j, 0)),
            pl.BlockSpec((1, heads * V_DIM, s), lambda i, j: (i, j, 0)),
        ],
        out_specs=pl.BlockSpec((1, s, heads * V_DIM), lambda i, j: (i, 0, j)),
        out_shape=jax.ShapeDtypeStruct((b, s, B_WIDTH), BF16),
        scratch_shapes=[
            pltpu.VMEM((heads, tk, tq), F32), pltpu.VMEM((heads, tk, tq), F32),
            pltpu.VMEM((heads, 1, tq), F32),
            pltpu.VMEM((heads, V_DIM + ONES_ROWS, tq), F32),
        ],
        compiler_params=pltpu.CompilerParams(
            dimension_semantics=("arbitrary", "arbitrary"), vmem_limit_bytes=VMEM_LIMIT_BYTES),
        name="mla",
    )(qb_t, kb_t, vb_t)


def _gelu_tanh(t):
    return 0.5 * t * (1.0 + jnp.tanh(math.sqrt(2.0 / math.pi) * (t + 0.044715 * (t * t * t))))


def _post_kernel(x_ref, ya_ref, yb_ref, g_ref, p_ref, wa_ref, wb_ref, wout_ref, gpost_ref,
                 gmlp_ref, wup_ref, cw_ref, cb_ref, wdown_ref, gmlppost_ref, gple_ref,
                 wpleg_ref, wple_ref, o_ref, ubuf_ref, *, tm):
    carry_rows = 8

    @pl.when(pl.program_id(1) == 0)
    def _():
        ubuf_ref[0:carry_rows, :] = jnp.zeros((carry_rows, 2 * D_FF), F32)

    x = x_ref[0]
    g = g_ref[0]
    ma = _dot(ya_ref[0], wa_ref[...])
    mb = _dot(yb_ref[0], wb_ref[...])
    mixed = g[:, :D_MODEL].astype(F32) * ma + g[:, D_MODEL:].astype(F32) * mb
    x1 = x + _rms_rows(_dot(mixed.astype(BF16), wout_ref[...]), gpost_ref[...])

    h2 = _rms_rows(x1, gmlp_ref[...]).astype(BF16)
    u = _dot(h2, wup_ref[...])
    ubuf_ref[carry_rows:carry_rows + tm, :] = u
    u1 = ubuf_ref[carry_rows - 1:carry_rows - 1 + tm, :]
    u2 = ubuf_ref[carry_rows - 2:carry_rows - 2 + tm, :]
    cw = cw_ref[...]
    y = cw[0:1] * u2 + cw[1:2] * u1 + cw[2:3] * u + cb_ref[...]
    ubuf_ref[0:carry_rows, :] = ubuf_ref[tm:tm + carry_rows, :]
    act = _gelu_tanh(y[:, :D_FF]) * y[:, D_FF:]
    ff = _dot(act.astype(BF16), wdown_ref[...])
    x2 = x1 + _rms_rows(ff, gmlppost_ref[...])

    e = _dot(p_ref[0].astype(BF16), wple_ref[...])
    gate = _sigmoid(_dot(_rms_rows(x2, gple_ref[...]).astype(BF16), wpleg_ref[...]))
    o_ref[0] = x2 + gate * e


def _post(x, ya, yb, gates, p, wa, wb, wout, gpost, gmlp, wup, cw, cb, wdown, gmlppost, gple,
          wpleg, wple, *, tm):
    b, s, d = x.shape
    const = lambda a: pl.BlockSpec(a.shape, lambda i, j: (0,) * a.ndim,
                                   pipeline_mode=pl.Buffered(1))
    rows = lambda n: pl.BlockSpec((1, tm, n), lambda i, j: (i, j, 0))
    weights = (wa, wb, wout, gpost, gmlp, wup, cw, cb, wdown, gmlppost, gple, wpleg, wple)
    return pl.pallas_call(
        functools.partial(_post_kernel, tm=tm),
        grid=(b, s // tm),
        in_specs=[rows(d), rows(A_WIDTH), rows(B_WIDTH), rows(2 * d), rows(PLE_DIM)]
        + [const(w) for w in weights],
        out_specs=rows(d),
        out_shape=jax.ShapeDtypeStruct((b, s, d), F32),
        scratch_shapes=[pltpu.VMEM((tm + 8, 2 * D_FF), F32)],
        compiler_params=pltpu.CompilerParams(
            dimension_semantics=("arbitrary", "arbitrary"), vmem_limit_bytes=VMEM_LIMIT_BYTES),
        name="post",
    )(x, ya, yb, gates, p, *weights)


def _layer(x, p_i, pos_f, inva, invb, attn_pre_norm, attn_post_norm, w_in, b_gate, sinks,
           q_a_norm, w_uq, kv_a_norm, w_ukv, w_branch_a, w_branch_b, w_out, mlp_pre_norm,
           mlp_post_norm, w_up, conv_w, conv_b, w_down, ple_norm, w_ple_gate, w_ple):
    row = lambda v: v.reshape(1, -1)
    col = lambda v: v.reshape(-1, 1)
    wfm = w_in[:, :N_FM].T.astype(BF16)
    wg = w_in[:, N_FM:].astype(BF16)
    wuq = jnp.pad(w_uq.reshape(Q_LORA, B_HEADS, NOPE_DIM + ROPE_DIM),
                  ((0, 0), (0, 0), (0, B_QK_PAD - NOPE_DIM - ROPE_DIM)))
    wuq = wuq.reshape(Q_LORA, B_HEADS * B_QK_PAD).T.astype(BF16)
    wukv = w_ukv.reshape(KV_LORA, B_HEADS, NOPE_DIM + V_DIM)
    wukv = jnp.concatenate([wukv[:, :, :NOPE_DIM].reshape(KV_LORA, -1),
                            wukv[:, :, NOPE_DIM:].reshape(KV_LORA, -1)], axis=1)
    wukv = wukv.T.astype(BF16)

    qa_t, ka_t, va_t, qb_t, kb_t, vb_t, gates = _inproj(
        x, pos_f, row(attn_pre_norm), wfm, wg, row(b_gate), col(q_a_norm), col(kv_a_norm),
        wuq, wukv, inva, invb, tm=512)

    sink_rows = jnp.repeat(sinks.astype(F32), BLOCK).reshape(A_KV_HEADS, A_GROUP * BLOCK)
    ya = _swa(qa_t, ka_t, va_t, sink_rows)
    yb = _mla(qb_t, kb_t, vb_t, heads=4, tq=512, tk=256)

    return _post(
        x, ya, yb, gates, p_i, w_branch_a.astype(BF16), w_branch_b.astype(BF16),
        w_out.astype(BF16), row(attn_post_norm), row(mlp_pre_norm), w_up.astype(BF16),
        conv_w, row(conv_b), w_down.astype(BF16), row(mlp_post_norm), row(ple_norm),
        w_ple_gate.astype(BF16), w_ple.astype(BF16), tm=256)


def kernel(x, p, positions, attn_pre_norm, attn_post_norm, w_in, b_gate, sinks, q_a_norm, w_uq,
           kv_a_norm, w_ukv, w_branch_a, w_branch_b, w_out, mlp_pre_norm, mlp_post_norm, w_up,
           conv_w, conv_b, w_down, ple_norm, w_ple_gate, w_ple):
    b, s, _ = x.shape
    pos_f = positions.astype(F32).reshape(b, 1, s)
    inva = (ROPE_THETA ** (-(jnp.arange(0, A_HEAD_DIM, 2, dtype=F32) / A_HEAD_DIM))).reshape(-1, 1)
    invb = (ROPE_THETA ** (-(jnp.arange(0, ROPE_DIM, 2, dtype=F32) / ROPE_DIM))).reshape(-1, 1)
    params = (attn_pre_norm, attn_post_norm, w_in, b_gate, sinks, q_a_norm, w_uq, kv_a_norm,
              w_ukv, w_branch_a, w_branch_b, w_out, mlp_pre_norm, mlp_post_norm, w_up, conv_w,
              conv_b, w_down, ple_norm, w_ple_gate, w_ple)
    for i in range(p.shape[0]):
        x = _layer(x, p[i], pos_f, inva, invb, *(w[i] for w in params))
    return x
```

```python
import functools
import math

import jax
import jax.numpy as jnp
from jax import lax
from jax.experimental import pallas as pl
from jax.experimental.pallas import tpu as pltpu

D_MODEL = 1024
PLE_DIM = 256
ROPE_THETA = 10000.0
RMS_EPS = 1e-6
BLOCK = 128
A_HEADS = 8
A_KV_HEADS = 2
A_GROUP = A_HEADS // A_KV_HEADS
A_HEAD_DIM = 64
A_WIDTH = A_HEADS * A_HEAD_DIM
A_KV_WIDTH = A_KV_HEADS * A_HEAD_DIM
B_HEADS = 8
Q_LORA = 256
KV_LORA = 128
NOPE_DIM = 64
ROPE_DIM = 32
V_DIM = 64
B_WIDTH = B_HEADS * V_DIM
B_QK_PAD = 128
ONES_ROWS = 16
SWA_UNROLL = 4
D_FF = 2816
CONV_W = 3
N_FM = A_WIDTH + 2 * A_KV_WIDTH + Q_LORA + KV_LORA + ROPE_DIM
VMEM_LIMIT_BYTES = 58 * 1024 * 1024
LOG2_E = math.log2(math.e)

BF16 = jnp.bfloat16
F32 = jnp.float32
_TN = (((0,), (0,)), ((), ()))
_NT = (((1,), (1,)), ((), ()))


def _dot(a, b):
    return jnp.dot(a, b, preferred_element_type=F32)


def _rms_rows(t, g):
    return t * lax.rsqrt(jnp.mean(t * t, axis=-1, keepdims=True) + RMS_EPS) * g


def _rms_cols(t, g):
    return t * lax.rsqrt(jnp.mean(t * t, axis=0, keepdims=True) + RMS_EPS) * g


def _sigmoid(t):
    return 1.0 / (1.0 + jnp.exp(-t))


def _rope_rows(t1, t2, c, s):
    return t1 * c - t2 * s, t2 * c + t1 * s


def _inproj_kernel(x_ref, pos_ref, gpre_ref, wfm_ref, wg_ref, bg_ref, qng_ref, kvng_ref,
                   wuq_ref, wukv_ref, inva_ref, invb_ref,
                   qa_ref, ka_ref, va_ref, qb_ref, kb_ref, vb_ref, gates_ref, *, splits):
    tm = x_ref.shape[1]
    tw = tm // splits
    parts = [slice(i * tw, (i + 1) * tw) for i in range(splits)]
    half_a = A_HEAD_DIM // 2
    half_b = ROPE_DIM // 2
    scale_a = LOG2_E * A_HEAD_DIM ** -0.5
    scale_b = LOG2_E * (NOPE_DIM + ROPE_DIM) ** -0.5
    zeros_pad = jnp.zeros((B_QK_PAD - NOPE_DIM - ROPE_DIM, tw), BF16)

    hs = [_rms_rows(x_ref[0, t, :], gpre_ref[...]).astype(BF16) for t in parts]
    fms = [lax.dot_general(wfm_ref[...], h, _NT, preferred_element_type=F32) for h in hs]

    for t, h in zip(parts, hs):
        gates = _dot(h, wg_ref[...]) + bg_ref[...]
        gates_ref[0, t, :] = _sigmoid(gates).astype(BF16)

    trig = []
    for t in parts:
        pos = pos_ref[0, :, t]
        ang_a = pos * inva_ref[...]
        ang_b = pos * invb_ref[...]
        trig.append((jnp.cos(ang_a), jnp.sin(ang_a), jnp.cos(ang_b), jnp.sin(ang_b)))

    for t, fm, (ca, sa, _, _) in zip(parts, fms, trig):
        for hd in range(A_HEADS):
            r = hd * A_HEAD_DIM
            o1, o2 = _rope_rows(fm[r:r + half_a], fm[r + half_a:r + A_HEAD_DIM], ca, sa)
            qa_ref[0, r:r + half_a, t] = (o1 * scale_a).astype(BF16)
            qa_ref[0, r + half_a:r + A_HEAD_DIM, t] = (o2 * scale_a).astype(BF16)
        for hd in range(A_KV_HEADS):
            r = hd * A_HEAD_DIM
            f = A_WIDTH + r
            o1, o2 = _rope_rows(fm[f:f + half_a], fm[f + half_a:f + A_HEAD_DIM], ca, sa)
            ka_ref[0, r:r + half_a, t] = o1.astype(BF16)
            ka_ref[0, r + half_a:r + A_HEAD_DIM, t] = o2.astype(BF16)
        f = A_WIDTH + A_KV_WIDTH
        va_ref[0, :, t] = fm[f:f + A_KV_WIDTH].astype(BF16)

    f_cq = A_WIDTH + 2 * A_KV_WIDTH
    f_ckv = f_cq + Q_LORA
    f_kr = f_ckv + KV_LORA
    qbs = [_dot(wuq_ref[...], _rms_cols(fm[f_cq:f_ckv], qng_ref[...]).astype(BF16)) * scale_b
           for fm in fms]
    kvs = [_dot(wukv_ref[...], _rms_cols(fm[f_ckv:f_kr], kvng_ref[...]).astype(BF16))
           for fm in fms]

    for t, fm, qb, kv, (_, _, cb, sb) in zip(parts, fms, qbs, kvs, trig):
        kr = fm[f_kr:f_kr + ROPE_DIM]
        k1, k2 = _rope_rows(kr[:half_b], kr[half_b:], cb, sb)
        k1 = k1.astype(BF16)
        k2 = k2.astype(BF16)
        for hd in range(B_HEADS):
            r = hd * B_QK_PAD
            p1 = r + NOPE_DIM
            qb_ref[0, r:p1, t] = qb[r:p1].astype(BF16)
            o1, o2 = _rope_rows(qb[p1:p1 + half_b], qb[p1 + half_b:p1 + ROPE_DIM], cb, sb)
            qb_ref[0, p1:p1 + half_b, t] = o1.astype(BF16)
            qb_ref[0, p1 + half_b:p1 + ROPE_DIM, t] = o2.astype(BF16)
            qb_ref[0, p1 + ROPE_DIM:r + B_QK_PAD, t] = zeros_pad
            kb_ref[0, r:p1, t] = kv[hd * NOPE_DIM:(hd + 1) * NOPE_DIM].astype(BF16)
            kb_ref[0, p1:p1 + half_b, t] = k1
            kb_ref[0, p1 + half_b:p1 + ROPE_DIM, t] = k2
            kb_ref[0, p1 + ROPE_DIM:r + B_QK_PAD, t] = zeros_pad
        vb_ref[0, :, t] = kv[B_HEADS * NOPE_DIM:].astype(BF16)


def _inproj(x, pos, gpre, wfm, wg, bg, qng, kvng, wuq, wukv, inva, invb, *, tm, splits):
    b, s, d = x.shape
    const = lambda shape: pl.BlockSpec(shape, lambda i, j: (0,) * len(shape))
    fm_out = lambda rows: pl.BlockSpec((1, rows, tm), lambda i, j: (i, 0, j))
    out_shape = (
        jax.ShapeDtypeStruct((b, A_WIDTH, s), BF16),
        jax.ShapeDtypeStruct((b, A_KV_WIDTH, s), BF16),
        jax.ShapeDtypeStruct((b, A_KV_WIDTH, s), BF16),
        jax.ShapeDtypeStruct((b, B_HEADS * B_QK_PAD, s), BF16),
        jax.ShapeDtypeStruct((b, B_HEADS * B_QK_PAD, s), BF16),
        jax.ShapeDtypeStruct((b, B_WIDTH, s), BF16),
        jax.ShapeDtypeStruct((b, s, 2 * d), BF16),
    )
    return pl.pallas_call(
        functools.partial(_inproj_kernel, splits=splits),
        grid=(b, s // tm),
        in_specs=[
            pl.BlockSpec((1, tm, d), lambda i, j: (i, j, 0)),
            pl.BlockSpec((1, 1, tm), lambda i, j: (i, 0, j)),
            const(gpre.shape), const(wfm.shape), const(wg.shape), const(bg.shape),
            const(qng.shape), const(kvng.shape), const(wuq.shape), const(wukv.shape),
            const(inva.shape), const(invb.shape),
        ],
        out_specs=(
            fm_out(A_WIDTH), fm_out(A_KV_WIDTH), fm_out(A_KV_WIDTH),
            fm_out(B_HEADS * B_QK_PAD), fm_out(B_HEADS * B_QK_PAD), fm_out(B_WIDTH),
            pl.BlockSpec((1, tm, 2 * d), lambda i, j: (i, j, 0)),
        ),
        out_shape=out_shape,
        compiler_params=pltpu.CompilerParams(
            dimension_semantics=("arbitrary", "arbitrary"), vmem_limit_bytes=VMEM_LIMIT_BYTES),
        name="inproj",
    )(x, pos, gpre, wfm, wg, bg, qng, kvng, wuq, wukv, inva, invb)


def _swa_kernel(q_ref, k_ref, v_ref, sink_ref, o_ref):
    seq = q_ref.shape[2]
    cols = A_GROUP * BLOCK
    ones = jnp.ones((ONES_ROWS, BLOCK), BF16)

    def blocks(starts, first):
        win = BLOCK if first else 2 * BLOCK
        kj = lax.broadcasted_iota(jnp.int32, (BLOCK, cols), 0)
        qi = lax.broadcasted_iota(jnp.int32, (BLOCK, cols), 1) & (BLOCK - 1)
        lower = kj <= qi
        chains = [(qs, kvh) for qs in starts for kvh in range(A_KV_HEADS)]

        def key_start(qs):
            if first or isinstance(qs, int):
                return 0 if first else qs - BLOCK
            return pl.multiple_of(qs - BLOCK, BLOCK)

        scores = []
        for qs, kvh in chains:
            r = kvh * A_HEAD_DIM
            k_t = k_ref[0, r:r + A_HEAD_DIM, pl.ds(key_start(qs), win)]
            q_t = jnp.concatenate(
                [q_ref[0, (kvh * A_GROUP + g) * A_HEAD_DIM:(kvh * A_GROUP + g + 1) * A_HEAD_DIM,
                       pl.ds(qs, BLOCK)] for g in range(A_GROUP)], axis=1)
            scores.append(lax.dot_general(k_t, q_t, _TN, preferred_element_type=F32))

        probs = []
        for (qs, kvh), s_t in zip(chains, scores):
            if first:
                merged = jnp.where(lower, s_t, -jnp.inf)
            else:
                merged = jnp.where(lower, s_t[BLOCK:], s_t[:BLOCK])
            sink = sink_ref[kvh:kvh + 1, :] * LOG2_E
            m = jnp.maximum(jnp.max(merged, axis=0, keepdims=True), sink)
            e = jnp.exp2(merged - m)
            p_t = jnp.where(lower, e, 0.0).astype(BF16)
            if not first:
                p_t = jnp.concatenate([jnp.where(lower, 0.0, e).astype(BF16), p_t], axis=0)
            probs.append((p_t, jnp.exp2(sink - m)))

        outs = []
        for (qs, kvh), (p_t, sink_e) in zip(chains, probs):
            r = kvh * A_HEAD_DIM
            v_t = v_ref[0, r:r + A_HEAD_DIM, pl.ds(key_start(qs), win)]
            v_aug = jnp.concatenate([v_t] + [jnp.concatenate([ones] * (win // BLOCK), axis=1)],
                                    axis=0)
            o_aug = _dot(v_aug, p_t)
            denom = o_aug[A_HEAD_DIM:A_HEAD_DIM + 1] + sink_e
            outs.append(o_aug[:A_HEAD_DIM] * (1.0 / denom))

        for (qs, kvh), o_t in zip(chains, outs):
            for j in range(A_GROUP // 2):
                pair = jnp.concatenate(
                    [o_t[:, (2 * j) * BLOCK:(2 * j + 1) * BLOCK],
                     o_t[:, (2 * j + 1) * BLOCK:(2 * j + 2) * BLOCK]], axis=0)
                c = (kvh * (A_GROUP // 2) + j) * 2 * A_HEAD_DIM
                o_ref[0, pl.ds(qs, BLOCK), c:c + 2 * A_HEAD_DIM] = pair.T.astype(BF16)

    blocks([0], True)
    blocks([u * BLOCK for u in range(1, SWA_UNROLL)], False)

    def body(i, carry):
        qs = pl.multiple_of(i * (SWA_UNROLL * BLOCK), SWA_UNROLL * BLOCK)
        blocks([pl.multiple_of(qs + u * BLOCK, BLOCK) for u in range(SWA_UNROLL)], False)
        return carry

    assert (seq // BLOCK) % SWA_UNROLL == 0
    lax.fori_loop(1, seq // (SWA_UNROLL * BLOCK), body, 0)


def _swa(qa_t, ka_t, va_t, sink_rows):
    b, _, s = qa_t.shape
    return pl.pallas_call(
        _swa_kernel,
        grid=(b,),
        in_specs=[
            pl.BlockSpec((1, A_WIDTH, s), lambda i: (i, 0, 0)),
            pl.BlockSpec((1, A_KV_WIDTH, s), lambda i: (i, 0, 0)),
            pl.BlockSpec((1, A_KV_WIDTH, s), lambda i: (i, 0, 0)),
            pl.BlockSpec(sink_rows.shape, lambda i: (0, 0)),
        ],
        out_specs=pl.BlockSpec((1, s, A_WIDTH), lambda i: (i, 0, 0)),
        out_shape=jax.ShapeDtypeStruct((b, s, A_WIDTH), BF16),
        compiler_params=pltpu.CompilerParams(
            dimension_semantics=("arbitrary",), vmem_limit_bytes=VMEM_LIMIT_BYTES),
        name="swa",
    )(qa_t, ka_t, va_t, sink_rows)


def _mla_kernel(q_ref, k_ref, v_ref, o_ref, sa_ref, sb_ref, m_ref, acc_ref, *, heads, tq, tk):
    seq = q_ref.shape[2]
    n_q = seq // tq
    ones = jnp.ones((ONES_ROWS, tk), BF16)

    def scores_to(buf, q0, k0, qoff, qw):
        for hh in range(heads):
            r = hh * B_QK_PAD
            k_t = k_ref[0, r:r + B_QK_PAD, pl.ds(k0, tk)]
            q_t = q_ref[0, r:r + B_QK_PAD, pl.ds(q0 + qoff, qw)]
            buf[hh, :, qoff:qoff + qw] = lax.dot_general(
                k_t, q_t, _TN, preferred_element_type=F32)

    def consume(buf, k0, qoff, qw, mask=None):
        for hh in range(heads):
            s_t = buf[hh, :, qoff:qoff + qw]
            if mask is not None:
                s_t = jnp.where(mask, s_t, -jnp.inf)
            m_prev = m_ref[hh, :, qoff:qoff + qw]
            m_new = jnp.maximum(m_prev, jnp.max(s_t, axis=0, keepdims=True))
            alpha = jnp.exp2(m_prev - m_new)
            p = jnp.exp2(s_t - m_new).astype(BF16)
            v_t = jnp.concatenate(
                [v_ref[0, hh * V_DIM:(hh + 1) * V_DIM, pl.ds(k0, tk)], ones], axis=0)
            acc_ref[hh, :, qoff:qoff + qw] = alpha * acc_ref[hh, :, qoff:qoff + qw] + _dot(v_t, p)
            m_ref[hh, :, qoff:qoff + qw] = m_new

    def causal(width):
        kj = lax.broadcasted_iota(jnp.int32, (tk, width), 0)
        qi = lax.broadcasted_iota(jnp.int32, (tk, width), 1)
        return kj <= qi

    def q_tile(qt, carry):
        q0 = pl.multiple_of(qt * tq, tq)
        m_ref[...] = jnp.full(m_ref.shape, -jnp.inf, F32)
        acc_ref[...] = jnp.zeros(acc_ref.shape, F32)

        def pair(j, c):
            k0 = pl.multiple_of(j * tq, tq)
            scores_to(sb_ref, q0, k0 + tk, 0, tq)
            consume(sa_ref, k0, 0, tq)
            scores_to(sa_ref, q0, k0 + 2 * tk, 0, tq)
            consume(sb_ref, k0 + tk, 0, tq)
            return c

        lax.fori_loop(0, qt, pair, 0)

        scores_to(sb_ref, q0, q0 + tk, tk, tq - tk)
        consume(sa_ref, q0, 0, tq, mask=causal(tq))
        q_next = pl.multiple_of(jnp.minimum(qt + 1, n_q - 1) * tq, tq)
        scores_to(sa_ref, q_next, 0, 0, tq)
        consume(sb_ref, q0 + tk, tk, tq - tk, mask=causal(tq - tk))

        outs = [acc_ref[hh, :V_DIM, :] * (1.0 / acc_ref[hh, V_DIM:V_DIM + 1, :])
                for hh in range(heads)]
        for hp in range(heads // 2):
            pair_t = jnp.concatenate(outs[2 * hp:2 * hp + 2], axis=0)
            o_ref[0, pl.ds(q0, tq), hp * 2 * V_DIM:(hp + 1) * 2 * V_DIM] = pair_t.T.astype(BF16)
        return carry

    scores_to(sa_ref, 0, 0, 0, tq)
    lax.fori_loop(0, n_q, q_tile, 0)


def _mla(qb_t, kb_t, vb_t, *, heads, tq, tk):
    assert tq == 2 * tk and heads % 2 == 0 and B_HEADS % heads == 0
    b, _, s = qb_t.shape
    return pl.pallas_call(
        functools.partial(_mla_kernel, heads=heads, tq=tq, tk=tk),
        grid=(b, B_HEADS // heads),
        in_specs=[
            pl.BlockSpec((1, heads * B_QK_PAD, s), lambda i, j: (i, j, 0)),
            pl.BlockSpec((1, heads * B_QK_PAD, s), lambda i, j: (i, j, 0)),
            pl.BlockSpec((1, heads * V_DIM, s), lambda i, j: (i, j, 0)),
        ],
        out_specs=pl.BlockSpec((1, s, heads * V_DIM), lambda i, j: (i, 0, j)),
        out_shape=jax.ShapeDtypeStruct((b, s, B_WIDTH), BF16),
        scratch_shapes=[
            pltpu.VMEM((heads, tk, tq), F32), pltpu.VMEM((heads, tk, tq), F32),
            pltpu.VMEM((heads, 1, tq), F32),
            pltpu.VMEM((heads, V_DIM + ONES_ROWS, tq), F32),
        ],
        compiler_params=pltpu.CompilerParams(
            dimension_semantics=("arbitrary", "arbitrary"), vmem_limit_bytes=VMEM_LIMIT_BYTES),
        name="mla",
    )(qb_t, kb_t, vb_t)


def _gelu_tanh(t):
    k1 = -2.0 * math.sqrt(2.0 / math.pi) * LOG2_E
    k3 = k1 * 0.044715
    return t * (1.0 / (1.0 + jnp.exp2(t * (k3 * (t * t) + k1))))


def _post_kernel(x_ref, ya_ref, yb_ref, g_ref, p_ref, wa_ref, wb_ref, wout_ref, gpost_ref,
                 gmlp_ref, wup_ref, cw_ref, cb_ref, wdown_ref, gmlppost_ref, gple_ref,
                 wpleg_ref, wple_ref, o_ref, ubuf_ref, *, tm):
    carry_rows = 8

    @pl.when(pl.program_id(1) == 0)
    def _():
        ubuf_ref[0:carry_rows, :] = jnp.zeros((carry_rows, 2 * D_FF), F32)

    x = x_ref[0]
    g = g_ref[0]
    ma = _dot(ya_ref[0], wa_ref[...])
    mb = _dot(yb_ref[0], wb_ref[...])
    mixed = g[:, :D_MODEL].astype(F32) * ma + g[:, D_MODEL:].astype(F32) * mb
    x1 = x + _rms_rows(_dot(mixed.astype(BF16), wout_ref[...]), gpost_ref[...])

    h2 = _rms_rows(x1, gmlp_ref[...]).astype(BF16)
    u = _dot(h2, wup_ref[...])
    ubuf_ref[carry_rows:carry_rows + tm, :] = u
    u1 = ubuf_ref[carry_rows - 1:carry_rows - 1 + tm, :]
    u2 = ubuf_ref[carry_rows - 2:carry_rows - 2 + tm, :]
    cw = cw_ref[...]
    y = cw[0:1] * u2 + cw[1:2] * u1 + cw[2:3] * u + cb_ref[...]
    ubuf_ref[0:carry_rows, :] = ubuf_ref[tm:tm + carry_rows, :]
    act = _gelu_tanh(y[:, :D_FF]) * y[:, D_FF:]
    ff = _dot(act.astype(BF16), wdown_ref[...])
    x2 = x1 + _rms_rows(ff, gmlppost_ref[...])

    e = _dot(p_ref[0].astype(BF16), wple_ref[...])
    gate = _sigmoid(_dot(_rms_rows(x2, gple_ref[...]).astype(BF16), wpleg_ref[...]))
    o_ref[0] = x2 + gate * e


def _post(x, ya, yb, gates, p, wa, wb, wout, gpost, gmlp, wup, cw, cb, wdown, gmlppost, gple,
          wpleg, wple, *, tm):
    b, s, d = x.shape
    const = lambda a: pl.BlockSpec(a.shape, lambda i, j: (0,) * a.ndim,
                                   pipeline_mode=pl.Buffered(1))
    rows = lambda n: pl.BlockSpec((1, tm, n), lambda i, j: (i, j, 0))
    weights = (wa, wb, wout, gpost, gmlp, wup, cw, cb, wdown, gmlppost, gple, wpleg, wple)
    return pl.pallas_call(
        functools.partial(_post_kernel, tm=tm),
        grid=(b, s // tm),
        in_specs=[rows(d), rows(A_WIDTH), rows(B_WIDTH), rows(2 * d), rows(PLE_DIM)]
        + [const(w) for w in weights],
        out_specs=rows(d),
        out_shape=jax.ShapeDtypeStruct((b, s, d), F32),
        scratch_shapes=[pltpu.VMEM((tm + 8, 2 * D_FF), F32)],
        compiler_params=pltpu.CompilerParams(
            dimension_semantics=("arbitrary", "arbitrary"), vmem_limit_bytes=VMEM_LIMIT_BYTES),
        name="post",
    )(x, ya, yb, gates, p, *weights)


def _layer(x, p_i, pos_f, inva, invb, attn_pre_norm, attn_post_norm, w_in, b_gate, sinks,
           q_a_norm, w_uq, kv_a_norm, w_ukv, w_branch_a, w_branch_b, w_out, mlp_pre_norm,
           mlp_post_norm, w_up, conv_w, conv_b, w_down, ple_norm, w_ple_gate, w_ple):
    row = lambda v: v.reshape(1, -1)
    col = lambda v: v.reshape(-1, 1)
    wfm = w_in[:, :N_FM].T.astype(BF16)
    wg = w_in[:, N_FM:].astype(BF16)
    wuq = jnp.pad(w_uq.reshape(Q_LORA, B_HEADS, NOPE_DIM + ROPE_DIM),
                  ((0, 0), (0, 0), (0, B_QK_PAD - NOPE_DIM - ROPE_DIM)))
    wuq = wuq.reshape(Q_LORA, B_HEADS * B_QK_PAD).T.astype(BF16)
    wukv = w_ukv.reshape(KV_LORA, B_HEADS, NOPE_DIM + V_DIM)
    wukv = jnp.concatenate([wukv[:, :, :NOPE_DIM].reshape(KV_LORA, -1),
                            wukv[:, :, NOPE_DIM:].reshape(KV_LORA, -1)], axis=1)
    wukv = wukv.T.astype(BF16)

    qa_t, ka_t, va_t, qb_t, kb_t, vb_t, gates = _inproj(
        x, pos_f, row(attn_pre_norm), wfm, wg, row(b_gate), col(q_a_norm), col(kv_a_norm),
        wuq, wukv, inva, invb, tm=512, splits=2)

    sink_rows = jnp.repeat(sinks.astype(F32), BLOCK).reshape(A_KV_HEADS, A_GROUP * BLOCK)
    ya = _swa(qa_t, ka_t, va_t, sink_rows)
    yb = _mla(qb_t, kb_t, vb_t, heads=4, tq=512, tk=256)

    return _post(
        x, ya, yb, gates, p_i, w_branch_a.astype(BF16), w_branch_b.astype(BF16),
        w_out.astype(BF16), row(attn_post_norm), row(mlp_pre_norm), w_up.astype(BF16),
        conv_w, row(conv_b), w_down.astype(BF16), row(mlp_post_norm), row(ple_norm),
        w_ple_gate.astype(BF16), w_ple.astype(BF16), tm=256)


def kernel(x, p, positions, attn_pre_norm, attn_post_norm, w_in, b_gate, sinks, q_a_norm, w_uq,
           kv_a_norm, w_ukv, w_branch_a, w_branch_b, w_out, mlp_pre_norm, mlp_post_norm, w_up,
           conv_w, conv_b, w_down, ple_norm, w_ple_gate, w_ple):
    b, s, _ = x.shape
    pos_f = positions.astype(F32).reshape(b, 1, s)
    inva = (ROPE_THETA ** (-(jnp.arange(0, A_HEAD_DIM, 2, dtype=F32) / A_HEAD_DIM))).reshape(-1, 1)
    invb = (ROPE_THETA ** (-(jnp.arange(0, ROPE_DIM, 2, dtype=F32) / ROPE_DIM))).reshape(-1, 1)
    params = (attn_pre_norm, attn_post_norm, w_in, b_gate, sinks, q_a_norm, w_uq, kv_a_norm,
              w_ukv, w_branch_a, w_branch_b, w_out, mlp_pre_norm, mlp_post_norm, w_up, conv_w,
              conv_b, w_down, ple_norm, w_ple_gate, w_ple)
    for i in range(p.shape[0]):
        x = _layer(x, p[i], pos_f, inva, invb, *(w[i] for w in params))
    return x
```

```python
import functools
import math

import jax
import jax.numpy as jnp
from jax import lax
from jax.experimental import pallas as pl
from jax.experimental.pallas import tpu as pltpu

D_MODEL = 1024
PLE_DIM = 256
ROPE_THETA = 10000.0
RMS_EPS = 1e-6
BLOCK = 128
A_HEADS = 8
A_KV_HEADS = 2
A_GROUP = A_HEADS // A_KV_HEADS
A_HEAD_DIM = 64
A_WIDTH = A_HEADS * A_HEAD_DIM
A_KV_WIDTH = A_KV_HEADS * A_HEAD_DIM
B_HEADS = 8
Q_LORA = 256
KV_LORA = 128
NOPE_DIM = 64
ROPE_DIM = 32
V_DIM = 64
B_WIDTH = B_HEADS * V_DIM
B_QK_PAD = 128
ONES_ROWS = 16
PAIR_UNROLL = 2
SWA_UNROLL = 4
D_FF = 2816
CONV_W = 3
N_FM = A_WIDTH + 2 * A_KV_WIDTH + Q_LORA + KV_LORA + ROPE_DIM
VMEM_LIMIT_BYTES = 58 * 1024 * 1024
LOG2_E = math.log2(math.e)

BF16 = jnp.bfloat16
F32 = jnp.float32
_TN = (((0,), (0,)), ((), ()))
_NT = (((1,), (1,)), ((), ()))


def _dot(a, b):
    return jnp.dot(a, b, preferred_element_type=F32)


def _rms_rows(t, g):
    return t * lax.rsqrt(jnp.mean(t * t, axis=-1, keepdims=True) + RMS_EPS) * g


def _rms_cols(t, g):
    return t * lax.rsqrt(jnp.mean(t * t, axis=0, keepdims=True) + RMS_EPS) * g


def _sigmoid(t):
    return 1.0 / (1.0 + jnp.exp(-t))


def _rope_rows(t1, t2, c, s):
    return t1 * c - t2 * s, t2 * c + t1 * s


def _inproj_kernel(x_ref, pos_ref, gpre_ref, wfm_ref, wg_ref, bg_ref, qng_ref, kvng_ref,
                   wuq_ref, wukv_ref, inva_ref, invb_ref,
                   qa_ref, ka_ref, va_ref, qb_ref, kb_ref, vb_ref, gates_ref, *, splits):
    tm = x_ref.shape[1]
    tw = tm // splits
    parts = [slice(i * tw, (i + 1) * tw) for i in range(splits)]
    half_a = A_HEAD_DIM // 2
    half_b = ROPE_DIM // 2
    scale_a = LOG2_E * A_HEAD_DIM ** -0.5
    scale_b = LOG2_E * (NOPE_DIM + ROPE_DIM) ** -0.5
    zeros_pad = jnp.zeros((B_QK_PAD - NOPE_DIM - ROPE_DIM, tw), BF16)

    hs = [_rms_rows(x_ref[0, t, :], gpre_ref[...]).astype(BF16) for t in parts]
    fms = [lax.dot_general(wfm_ref[...], h, _NT, preferred_element_type=F32) for h in hs]

    for t, h in zip(parts, hs):
        gates = _dot(h, wg_ref[...]) + bg_ref[...]
        gates_ref[0, t, :] = _sigmoid(gates).astype(BF16)

    trig = []
    for t in parts:
        pos = pos_ref[0, :, t]
        ang_a = pos * inva_ref[...]
        ang_b = pos * invb_ref[...]
        trig.append((jnp.cos(ang_a), jnp.sin(ang_a), jnp.cos(ang_b), jnp.sin(ang_b)))

    for t, fm, (ca, sa, _, _) in zip(parts, fms, trig):
        for hd in range(A_HEADS):
            r = hd * A_HEAD_DIM
            o1, o2 = _rope_rows(fm[r:r + half_a], fm[r + half_a:r + A_HEAD_DIM], ca, sa)
            qa_ref[0, r:r + half_a, t] = (o1 * scale_a).astype(BF16)
            qa_ref[0, r + half_a:r + A_HEAD_DIM, t] = (o2 * scale_a).astype(BF16)
        for hd in range(A_KV_HEADS):
            r = hd * A_HEAD_DIM
            f = A_WIDTH + r
            o1, o2 = _rope_rows(fm[f:f + half_a], fm[f + half_a:f + A_HEAD_DIM], ca, sa)
            ka_ref[0, r:r + half_a, t] = o1.astype(BF16)
            ka_ref[0, r + half_a:r + A_HEAD_DIM, t] = o2.astype(BF16)
        f = A_WIDTH + A_KV_WIDTH
        va_ref[0, :, t] = fm[f:f + A_KV_WIDTH].astype(BF16)

    f_cq = A_WIDTH + 2 * A_KV_WIDTH
    f_ckv = f_cq + Q_LORA
    f_kr = f_ckv + KV_LORA
    qbs = [_dot(wuq_ref[...], _rms_cols(fm[f_cq:f_ckv], qng_ref[...]).astype(BF16)) * scale_b
           for fm in fms]
    kvs = [_dot(wukv_ref[...], _rms_cols(fm[f_ckv:f_kr], kvng_ref[...]).astype(BF16))
           for fm in fms]

    for t, fm, qb, kv, (_, _, cb, sb) in zip(parts, fms, qbs, kvs, trig):
        kr = fm[f_kr:f_kr + ROPE_DIM]
        k1, k2 = _rope_rows(kr[:half_b], kr[half_b:], cb, sb)
        k1 = k1.astype(BF16)
        k2 = k2.astype(BF16)
        for hd in range(B_HEADS):
            r = hd * B_QK_PAD
            p1 = r + NOPE_DIM
            qb_ref[0, r:p1, t] = qb[r:p1].astype(BF16)
            o1, o2 = _rope_rows(qb[p1:p1 + half_b], qb[p1 + half_b:p1 + ROPE_DIM], cb, sb)
            qb_ref[0, p1:p1 + half_b, t] = o1.astype(BF16)
            qb_ref[0, p1 + half_b:p1 + ROPE_DIM, t] = o2.astype(BF16)
            qb_ref[0, p1 + ROPE_DIM:r + B_QK_PAD, t] = zeros_pad
            kb_ref[0, r:p1, t] = kv[hd * NOPE_DIM:(hd + 1) * NOPE_DIM].astype(BF16)
            kb_ref[0, p1:p1 + half_b, t] = k1
            kb_ref[0, p1 + half_b:p1 + ROPE_DIM, t] = k2
            kb_ref[0, p1 + ROPE_DIM:r + B_QK_PAD, t] = zeros_pad
        vb_ref[0, :, t] = kv[B_HEADS * NOPE_DIM:].astype(BF16)


def _inproj(x, pos, gpre, wfm, wg, bg, qng, kvng, wuq, wukv, inva, invb, *, tm, splits):
    b, s, d = x.shape
    const = lambda shape: pl.BlockSpec(shape, lambda i, j: (0,) * len(shape))
    fm_out = lambda rows: pl.BlockSpec((1, rows, tm), lambda i, j: (i, 0, j))
    out_shape = (
        jax.ShapeDtypeStruct((b, A_WIDTH, s), BF16),
        jax.ShapeDtypeStruct((b, A_KV_WIDTH, s), BF16),
        jax.ShapeDtypeStruct((b, A_KV_WIDTH, s), BF16),
        jax.ShapeDtypeStruct((b, B_HEADS * B_QK_PAD, s), BF16),
        jax.ShapeDtypeStruct((b, B_HEADS * B_QK_PAD, s), BF16),
        jax.ShapeDtypeStruct((b, B_WIDTH, s), BF16),
        jax.ShapeDtypeStruct((b, s, 2 * d), BF16),
    )
    return pl.pallas_call(
        functools.partial(_inproj_kernel, splits=splits),
        grid=(b, s // tm),
        in_specs=[
            pl.BlockSpec((1, tm, d), lambda i, j: (i, j, 0)),
            pl.BlockSpec((1, 1, tm), lambda i, j: (i, 0, j)),
            const(gpre.shape), const(wfm.shape), const(wg.shape), const(bg.shape),
            const(qng.shape), const(kvng.shape), const(wuq.shape), const(wukv.shape),
            const(inva.shape), const(invb.shape),
        ],
        out_specs=(
            fm_out(A_WIDTH), fm_out(A_KV_WIDTH), fm_out(A_KV_WIDTH),
            fm_out(B_HEADS * B_QK_PAD), fm_out(B_HEADS * B_QK_PAD), fm_out(B_WIDTH),
            pl.BlockSpec((1, tm, 2 * d), lambda i, j: (i, j, 0)),
        ),
        out_shape=out_shape,
        compiler_params=pltpu.CompilerParams(
            dimension_semantics=("arbitrary", "arbitrary"), vmem_limit_bytes=VMEM_LIMIT_BYTES),
        name="inproj",
    )(x, pos, gpre, wfm, wg, bg, qng, kvng, wuq, wukv, inva, invb)


def _swa_kernel(q_ref, k_ref, v_ref, sink_ref, o_ref):
    seq = q_ref.shape[2]
    cols = A_GROUP * BLOCK
    ones = jnp.ones((ONES_ROWS, BLOCK), BF16)

    def blocks(starts, first):
        win = BLOCK if first else 2 * BLOCK
        kj = lax.broadcasted_iota(jnp.int32, (BLOCK, cols), 0)
        qi = lax.broadcasted_iota(jnp.int32, (BLOCK, cols), 1) & (BLOCK - 1)
        lower = kj <= qi
        chains = [(qs, kvh) for qs in starts for kvh in range(A_KV_HEADS)]

        def key_start(qs):
            if first or isinstance(qs, int):
                return 0 if first else qs - BLOCK
            return pl.multiple_of(qs - BLOCK, BLOCK)

        scores = []
        for qs, kvh in chains:
            r = kvh * A_HEAD_DIM
            k_t = k_ref[0, r:r + A_HEAD_DIM, pl.ds(key_start(qs), win)]
            q_t = jnp.concatenate(
                [q_ref[0, (kvh * A_GROUP + g) * A_HEAD_DIM:(kvh * A_GROUP + g + 1) * A_HEAD_DIM,
                       pl.ds(qs, BLOCK)] for g in range(A_GROUP)], axis=1)
            scores.append(lax.dot_general(k_t, q_t, _TN, preferred_element_type=F32))

        probs = []
        for (qs, kvh), s_t in zip(chains, scores):
            if first:
                merged = jnp.where(lower, s_t, -jnp.inf)
            else:
                merged = jnp.where(lower, s_t[BLOCK:], s_t[:BLOCK])
            sink = sink_ref[kvh:kvh + 1, :] * LOG2_E
            m = jnp.maximum(jnp.max(merged, axis=0, keepdims=True), sink)
            e = jnp.exp2(merged - m)
            p_t = jnp.where(lower, e, 0.0).astype(BF16)
            if not first:
                p_t = jnp.concatenate([jnp.where(lower, 0.0, e).astype(BF16), p_t], axis=0)
            probs.append((p_t, jnp.exp2(sink - m)))

        outs = []
        for (qs, kvh), (p_t, sink_e) in zip(chains, probs):
            r = kvh * A_HEAD_DIM
            v_t = v_ref[0, r:r + A_HEAD_DIM, pl.ds(key_start(qs), win)]
            v_aug = jnp.concatenate([v_t] + [jnp.concatenate([ones] * (win // BLOCK), axis=1)],
                                    axis=0)
            o_aug = _dot(v_aug, p_t)
            denom = o_aug[A_HEAD_DIM:A_HEAD_DIM + 1] + sink_e
            outs.append(o_aug[:A_HEAD_DIM] * (1.0 / denom))

        for (qs, kvh), o_t in zip(chains, outs):
            for j in range(A_GROUP // 2):
                pair = jnp.concatenate(
                    [o_t[:, (2 * j) * BLOCK:(2 * j + 1) * BLOCK],
                     o_t[:, (2 * j + 1) * BLOCK:(2 * j + 2) * BLOCK]], axis=0)
                c = (kvh * (A_GROUP // 2) + j) * 2 * A_HEAD_DIM
                o_ref[0, pl.ds(qs, BLOCK), c:c + 2 * A_HEAD_DIM] = pair.T.astype(BF16)

    blocks([0], True)
    blocks([u * BLOCK for u in range(1, SWA_UNROLL)], False)

    def body(i, carry):
        qs = pl.multiple_of(i * (SWA_UNROLL * BLOCK), SWA_UNROLL * BLOCK)
        blocks([pl.multiple_of(qs + u * BLOCK, BLOCK) for u in range(SWA_UNROLL)], False)
        return carry

    assert (seq // BLOCK) % SWA_UNROLL == 0
    lax.fori_loop(1, seq // (SWA_UNROLL * BLOCK), body, 0)


def _swa(qa_t, ka_t, va_t, sink_rows):
    b, _, s = qa_t.shape
    return pl.pallas_call(
        _swa_kernel,
        grid=(b,),
        in_specs=[
            pl.BlockSpec((1, A_WIDTH, s), lambda i: (i, 0, 0)),
            pl.BlockSpec((1, A_KV_WIDTH, s), lambda i: (i, 0, 0)),
            pl.BlockSpec((1, A_KV_WIDTH, s), lambda i: (i, 0, 0)),
            pl.BlockSpec(sink_rows.shape, lambda i: (0, 0)),
        ],
        out_specs=pl.BlockSpec((1, s, A_WIDTH), lambda i: (i, 0, 0)),
        out_shape=jax.ShapeDtypeStruct((b, s, A_WIDTH), BF16),
        compiler_params=pltpu.CompilerParams(
            dimension_semantics=("arbitrary",), vmem_limit_bytes=VMEM_LIMIT_BYTES),
        name="swa",
    )(qa_t, ka_t, va_t, sink_rows)


def _mla_kernel(q_ref, k_ref, v_ref, o_ref, sa_ref, sb_ref, m_ref, acc_ref, *, heads, tq, tk):
    seq = q_ref.shape[2]
    n_q = seq // tq
    ones = jnp.ones((ONES_ROWS, tk), BF16)

    def scores_to(buf, q0, k0, qoff, qw):
        for hh in range(heads):
            r = hh * B_QK_PAD
            k_t = k_ref[0, r:r + B_QK_PAD, pl.ds(k0, tk)]
            q_t = q_ref[0, r:r + B_QK_PAD, pl.ds(q0 + qoff, qw)]
            buf[hh, :, qoff:qoff + qw] = lax.dot_general(
                k_t, q_t, _TN, preferred_element_type=F32)

    def consume(buf, k0, qoff, qw, mask=None):
        for hh in range(heads):
            s_t = buf[hh, :, qoff:qoff + qw]
            if mask is not None:
                s_t = jnp.where(mask, s_t, -jnp.inf)
            m_prev = m_ref[hh, :, qoff:qoff + qw]
            m_new = jnp.maximum(m_prev, jnp.max(s_t, axis=0, keepdims=True))
            alpha = jnp.exp2(m_prev - m_new)
            p = jnp.exp2(s_t - m_new).astype(BF16)
            v_t = jnp.concatenate(
                [v_ref[0, hh * V_DIM:(hh + 1) * V_DIM, pl.ds(k0, tk)], ones], axis=0)
            acc_ref[hh, :, qoff:qoff + qw] = alpha * acc_ref[hh, :, qoff:qoff + qw] + _dot(v_t, p)
            m_ref[hh, :, qoff:qoff + qw] = m_new

    def causal(width):
        kj = lax.broadcasted_iota(jnp.int32, (tk, width), 0)
        qi = lax.broadcasted_iota(jnp.int32, (tk, width), 1)
        return kj <= qi

    def q_tile(qt, carry):
        q0 = pl.multiple_of(qt * tq, tq)
        m_ref[...] = jnp.full(m_ref.shape, -jnp.inf, F32)
        acc_ref[...] = jnp.zeros(acc_ref.shape, F32)

        def pair(j):
            k0 = pl.multiple_of(j * tq, tq)
            scores_to(sb_ref, q0, k0 + tk, 0, tq)
            consume(sa_ref, k0, 0, tq)
            scores_to(sa_ref, q0, k0 + 2 * tk, 0, tq)
            consume(sb_ref, k0 + tk, 0, tq)

        def pairs(i, c):
            for u in range(PAIR_UNROLL):
                pair(i * PAIR_UNROLL + u)
            return c

        lax.fori_loop(0, qt // PAIR_UNROLL, pairs, 0)
        for u in range(PAIR_UNROLL - 1):
            @pl.when(u < qt % PAIR_UNROLL)
            def _(u=u):
                pair(qt - qt % PAIR_UNROLL + u)

        scores_to(sb_ref, q0, q0 + tk, tk, tq - tk)
        consume(sa_ref, q0, 0, tq, mask=causal(tq))
        q_next = pl.multiple_of(jnp.minimum(qt + 1, n_q - 1) * tq, tq)
        scores_to(sa_ref, q_next, 0, 0, tq)
        consume(sb_ref, q0 + tk, tk, tq - tk, mask=causal(tq - tk))

        outs = [acc_ref[hh, :V_DIM, :] * (1.0 / acc_ref[hh, V_DIM:V_DIM + 1, :])
                for hh in range(heads)]
        for hp in range(heads // 2):
            pair_t = jnp.concatenate(outs[2 * hp:2 * hp + 2], axis=0)
            o_ref[0, pl.ds(q0, tq), hp * 2 * V_DIM:(hp + 1) * 2 * V_DIM] = pair_t.T.astype(BF16)
        return carry

    scores_to(sa_ref, 0, 0, 0, tq)
    lax.fori_loop(0, n_q, q_tile, 0)


def _mla(qb_t, kb_t, vb_t, *, heads, tq, tk):
    assert tq == 2 * tk and heads % 2 == 0 and B_HEADS % heads == 0
    b, _, s = qb_t.shape
    return pl.pallas_call(
        functools.partial(_mla_kernel, heads=heads, tq=tq, tk=tk),
        grid=(b, B_HEADS // heads),
        in_specs=[
            pl.BlockSpec((1, heads * B_QK_PAD, s), lambda i, j: (i, j, 0)),
            pl.BlockSpec((1, heads * B_QK_PAD, s), lambda i, j: (i, j, 0)),
            pl.BlockSpec((1, heads * V_DIM, s), lambda i, j: (i, j, 0)),
        ],
        out_specs=pl.BlockSpec((1, s, heads * V_DIM), lambda i, j: (i, 0, j)),
        out_shape=jax.ShapeDtypeStruct((b, s, B_WIDTH), BF16),
        scratch_shapes=[
            pltpu.VMEM((heads, tk, tq), F32), pltpu.VMEM((heads, tk, tq), F32),
            pltpu.VMEM((heads, 1, tq), F32),
            pltpu.VMEM((heads, V_DIM + ONES_ROWS, tq), F32),
        ],
        compiler_params=pltpu.CompilerParams(
            dimension_semantics=("arbitrary", "arbitrary"), vmem_limit_bytes=VMEM_LIMIT_BYTES),
        name="mla",
    )(qb_t, kb_t, vb_t)


def _gelu_tanh(t):
    k1 = -2.0 * math.sqrt(2.0 / math.pi) * LOG2_E
    k3 = k1 * 0.044715
    return t * (1.0 / (1.0 + jnp.exp2(t * (k3 * (t * t) + k1))))


def _post_kernel(x_ref, ya_ref, yb_ref, g_ref, p_ref, wa_ref, wb_ref, wout_ref, gpost_ref,
                 gmlp_ref, wup_ref, cw_ref, cb_ref, wdown_ref, gmlppost_ref, gple_ref,
                 wpleg_ref, wple_ref, o_ref, ubuf_ref, *, tm):
    carry_rows = 8

    @pl.when(pl.program_id(1) == 0)
    def _():
        ubuf_ref[0:carry_rows, :] = jnp.zeros((carry_rows, 2 * D_FF), F32)

    x = x_ref[0]
    g = g_ref[0]
    ma = _dot(ya_ref[0], wa_ref[...])
    mb = _dot(yb_ref[0], wb_ref[...])
    mixed = g[:, :D_MODEL].astype(F32) * ma + g[:, D_MODEL:].astype(F32) * mb
    x1 = x + _rms_rows(_dot(mixed.astype(BF16), wout_ref[...]), gpost_ref[...])

    h2 = _rms_rows(x1, gmlp_ref[...]).astype(BF16)
    u = _dot(h2, wup_ref[...])
    ubuf_ref[carry_rows:carry_rows + tm, :] = u
    u1 = ubuf_ref[carry_rows - 1:carry_rows - 1 + tm, :]
    u2 = ubuf_ref[carry_rows - 2:carry_rows - 2 + tm, :]
    cw = cw_ref[...]
    y = cw[0:1] * u2 + cw[1:2] * u1 + cw[2:3] * u + cb_ref[...]
    ubuf_ref[0:carry_rows, :] = ubuf_ref[tm:tm + carry_rows, :]
    act = _gelu_tanh(y[:, :D_FF]) * y[:, D_FF:]
    ff = _dot(act.astype(BF16), wdown_ref[...])
    x2 = x1 + _rms_rows(ff, gmlppost_ref[...])

    e = _dot(p_ref[0].astype(BF16), wple_ref[...])
    gate = _sigmoid(_dot(_rms_rows(x2, gple_ref[...]).astype(BF16), wpleg_ref[...]))
    o_ref[0] = x2 + gate * e


def _post(x, ya, yb, gates, p, wa, wb, wout, gpost, gmlp, wup, cw, cb, wdown, gmlppost, gple,
          wpleg, wple, *, tm):
    b, s, d = x.shape
    const = lambda a: pl.BlockSpec(a.shape, lambda i, j: (0,) * a.ndim,
                                   pipeline_mode=pl.Buffered(1))
    rows = lambda n: pl.BlockSpec((1, tm, n), lambda i, j: (i, j, 0))
    weights = (wa, wb, wout, gpost, gmlp, wup, cw, cb, wdown, gmlppost, gple, wpleg, wple)
    return pl.pallas_call(
        functools.partial(_post_kernel, tm=tm),
        grid=(b, s // tm),
        in_specs=[rows(d), rows(A_WIDTH), rows(B_WIDTH), rows(2 * d), rows(PLE_DIM)]
        + [const(w) for w in weights],
        out_specs=rows(d),
        out_shape=jax.ShapeDtypeStruct((b, s, d), F32),
        scratch_shapes=[pltpu.VMEM((tm + 8, 2 * D_FF), F32)],
        compiler_params=pltpu.CompilerParams(
            dimension_semantics=("arbitrary", "arbitrary"), vmem_limit_bytes=VMEM_LIMIT_BYTES),
        name="post",
    )(x, ya, yb, gates, p, *weights)


def _layer(x, p_i, pos_f, inva, invb, attn_pre_norm, attn_post_norm, w_in, b_gate, sinks,
           q_a_norm, w_uq, kv_a_norm, w_ukv, w_branch_a, w_branch_b, w_out, mlp_pre_norm,
           mlp_post_norm, w_up, conv_w, conv_b, w_down, ple_norm, w_ple_gate, w_ple):
    row = lambda v: v.reshape(1, -1)
    col = lambda v: v.reshape(-1, 1)
    wfm = w_in[:, :N_FM].T.astype(BF16)
    wg = w_in[:, N_FM:].astype(BF16)
    wuq = jnp.pad(w_uq.reshape(Q_LORA, B_HEADS, NOPE_DIM + ROPE_DIM),
                  ((0, 0), (0, 0), (0, B_QK_PAD - NOPE_DIM - ROPE_DIM)))
    wuq = wuq.reshape(Q_LORA, B_HEADS * B_QK_PAD).T.astype(BF16)
    wukv = w_ukv.reshape(KV_LORA, B_HEADS, NOPE_DIM + V_DIM)
    wukv = jnp.concatenate([wukv[:, :, :NOPE_DIM].reshape(KV_LORA, -1),
                            wukv[:, :, NOPE_DIM:].reshape(KV_LORA, -1)], axis=1)
    wukv = wukv.T.astype(BF16)

    qa_t, ka_t, va_t, qb_t, kb_t, vb_t, gates = _inproj(
        x, pos_f, row(attn_pre_norm), wfm, wg, row(b_gate), col(q_a_norm), col(kv_a_norm),
        wuq, wukv, inva, invb, tm=1024, splits=4)

    sink_rows = jnp.repeat(sinks.astype(F32), BLOCK).reshape(A_KV_HEADS, A_GROUP * BLOCK)
    ya = _swa(qa_t, ka_t, va_t, sink_rows)
    yb = _mla(qb_t, kb_t, vb_t, heads=4, tq=512, tk=256)

    return _post(
        x, ya, yb, gates, p_i, w_branch_a.astype(BF16), w_branch_b.astype(BF16),
        w_out.astype(BF16), row(attn_post_norm), row(mlp_pre_norm), w_up.astype(BF16),
        conv_w, row(conv_b), w_down.astype(BF16), row(mlp_post_norm), row(ple_norm),
        w_ple_gate.astype(BF16), w_ple.astype(BF16), tm=256)


def kernel(x, p, positions, attn_pre_norm, attn_post_norm, w_in, b_gate, sinks, q_a_norm, w_uq,
           kv_a_norm, w_ukv, w_branch_a, w_branch_b, w_out, mlp_pre_norm, mlp_post_norm, w_up,
           conv_w, conv_b, w_down, ple_norm, w_ple_gate, w_ple):
    b, s, _ = x.shape
    pos_f = positions.astype(F32).reshape(b, 1, s)
    inva = (ROPE_THETA ** (-(jnp.arange(0, A_HEAD_DIM, 2, dtype=F32) / A_HEAD_DIM))).reshape(-1, 1)
    invb = (ROPE_THETA ** (-(jnp.arange(0, ROPE_DIM, 2, dtype=F32) / ROPE_DIM))).reshape(-1, 1)
    params = (attn_pre_norm, attn_post_norm, w_in, b_gate, sinks, q_a_norm, w_uq, kv_a_norm,
              w_ukv, w_branch_a, w_branch_b, w_out, mlp_pre_norm, mlp_post_norm, w_up, conv_w,
              conv_b, w_down, ple_norm, w_ple_gate, w_ple)
    for i in range(p.shape[0]):
        x = _layer(x, p[i], pos_f, inva, invb, *(w[i] for w in params))
    return x
```

```python
import functools
import math

import jax
import jax.numpy as jnp
from jax import lax
from jax.experimental import pallas as pl
from jax.experimental.pallas import tpu as pltpu

D_MODEL = 1024
PLE_DIM = 256
ROPE_THETA = 10000.0
RMS_EPS = 1e-6
BLOCK = 128
A_HEADS = 8
A_KV_HEADS = 2
A_GROUP = A_HEADS // A_KV_HEADS
A_HEAD_DIM = 64
A_WIDTH = A_HEADS * A_HEAD_DIM
A_KV_WIDTH = A_KV_HEADS * A_HEAD_DIM
B_HEADS = 8
Q_LORA = 256
KV_LORA = 128
NOPE_DIM = 64
ROPE_DIM = 32
V_DIM = 64
B_WIDTH = B_HEADS * V_DIM
B_QK_PAD = 128
ONES_ROWS = 16
SWA_UNROLL = 8
D_FF = 2816
CONV_W = 3
N_FM = A_WIDTH + 2 * A_KV_WIDTH + Q_LORA + KV_LORA + ROPE_DIM
VMEM_LIMIT_BYTES = 58 * 1024 * 1024
LOG2_E = math.log2(math.e)

BF16 = jnp.bfloat16
F32 = jnp.float32
_TN = (((0,), (0,)), ((), ()))
_NT = (((1,), (1,)), ((), ()))


def _dot(a, b):
    return jnp.dot(a, b, preferred_element_type=F32)


def _rms_rows(t, g):
    return t * lax.rsqrt(jnp.mean(t * t, axis=-1, keepdims=True) + RMS_EPS) * g


def _rms_cols(t, g):
    return t * lax.rsqrt(jnp.mean(t * t, axis=0, keepdims=True) + RMS_EPS) * g


def _sigmoid(t):
    return 1.0 / (1.0 + jnp.exp(-t))


def _rope_rows(t1, t2, c, s):
    return t1 * c - t2 * s, t2 * c + t1 * s


def _inproj_kernel(x_ref, pos_ref, gpre_ref, wfm_ref, wg_ref, bg_ref, qng_ref, kvng_ref,
                   wuq_ref, wukv_ref, inva_ref, invb_ref,
                   qa_ref, ka_ref, va_ref, qb_ref, kb_ref, vb_ref, gates_ref, *, splits):
    tm = x_ref.shape[1]
    tw = tm // splits
    parts = [slice(i * tw, (i + 1) * tw) for i in range(splits)]
    half_a = A_HEAD_DIM // 2
    half_b = ROPE_DIM // 2
    scale_a = LOG2_E * A_HEAD_DIM ** -0.5
    scale_b = LOG2_E * (NOPE_DIM + ROPE_DIM) ** -0.5
    zeros_pad = jnp.zeros((B_QK_PAD - NOPE_DIM - ROPE_DIM, tw), BF16)

    hs = [_rms_rows(x_ref[0, t, :], gpre_ref[...]).astype(BF16) for t in parts]
    fms = [lax.dot_general(wfm_ref[...], h, _NT, preferred_element_type=F32) for h in hs]

    for t, h in zip(parts, hs):
        gates = _dot(h, wg_ref[...]) + bg_ref[...]
        gates_ref[0, t, :] = _sigmoid(gates).astype(BF16)

    trig = []
    for t in parts:
        pos = pos_ref[0, :, t]
        ang_a = pos * inva_ref[...]
        ang_b = pos * invb_ref[...]
        trig.append((jnp.cos(ang_a), jnp.sin(ang_a), jnp.cos(ang_b), jnp.sin(ang_b)))

    for t, fm, (ca, sa, _, _) in zip(parts, fms, trig):
        for hd in range(A_HEADS):
            r = hd * A_HEAD_DIM
            o1, o2 = _rope_rows(fm[r:r + half_a], fm[r + half_a:r + A_HEAD_DIM], ca, sa)
            qa_ref[0, r:r + half_a, t] = (o1 * scale_a).astype(BF16)
            qa_ref[0, r + half_a:r + A_HEAD_DIM, t] = (o2 * scale_a).astype(BF16)
        for hd in range(A_KV_HEADS):
            r = hd * A_HEAD_DIM
            f = A_WIDTH + r
            o1, o2 = _rope_rows(fm[f:f + half_a], fm[f + half_a:f + A_HEAD_DIM], ca, sa)
            ka_ref[0, r:r + half_a, t] = o1.astype(BF16)
            ka_ref[0, r + half_a:r + A_HEAD_DIM, t] = o2.astype(BF16)
        f = A_WIDTH + A_KV_WIDTH
        va_ref[0, :, t] = fm[f:f + A_KV_WIDTH].astype(BF16)

    f_cq = A_WIDTH + 2 * A_KV_WIDTH
    f_ckv = f_cq + Q_LORA
    f_kr = f_ckv + KV_LORA
    qbs = [_dot(wuq_ref[...], _rms_cols(fm[f_cq:f_ckv], qng_ref[...]).astype(BF16)) * scale_b
           for fm in fms]
    kvs = [_dot(wukv_ref[...], _rms_cols(fm[f_ckv:f_kr], kvng_ref[...]).astype(BF16))
           for fm in fms]

    for t, fm, qb, kv, (_, _, cb, sb) in zip(parts, fms, qbs, kvs, trig):
        kr = fm[f_kr:f_kr + ROPE_DIM]
        k1, k2 = _rope_rows(kr[:half_b], kr[half_b:], cb, sb)
        k_tail = jnp.concatenate(
            [k1, k2, jnp.zeros((B_QK_PAD - NOPE_DIM - ROPE_DIM, tw), F32)], axis=0)
        for hd in range(B_HEADS):
            r = hd * B_QK_PAD
            p1 = r + NOPE_DIM
            qb_ref[0, r:p1, t] = qb[r:p1].astype(BF16)
            o1, o2 = _rope_rows(qb[p1:p1 + half_b], qb[p1 + half_b:p1 + ROPE_DIM], cb, sb)
            qb_ref[0, p1:p1 + half_b, t] = o1.astype(BF16)
            qb_ref[0, p1 + half_b:p1 + ROPE_DIM, t] = o2.astype(BF16)
            qb_ref[0, p1 + ROPE_DIM:r + B_QK_PAD, t] = zeros_pad
            k_head = jnp.concatenate([kv[hd * NOPE_DIM:(hd + 1) * NOPE_DIM], k_tail], axis=0)
            kb_ref[0, t, r:r + B_QK_PAD] = k_head.T.astype(BF16)
        vb_ref[0, :, t] = kv[B_HEADS * NOPE_DIM:].astype(BF16)


def _inproj(x, pos, gpre, wfm, wg, bg, qng, kvng, wuq, wukv, inva, invb, *, tm, splits):
    b, s, d = x.shape
    const = lambda shape: pl.BlockSpec(shape, lambda i, j: (0,) * len(shape))
    fm_out = lambda rows: pl.BlockSpec((1, rows, tm), lambda i, j: (i, 0, j))
    out_shape = (
        jax.ShapeDtypeStruct((b, A_WIDTH, s), BF16),
        jax.ShapeDtypeStruct((b, A_KV_WIDTH, s), BF16),
        jax.ShapeDtypeStruct((b, A_KV_WIDTH, s), BF16),
        jax.ShapeDtypeStruct((b, B_HEADS * B_QK_PAD, s), BF16),
        jax.ShapeDtypeStruct((b, s, B_HEADS * B_QK_PAD), BF16),
        jax.ShapeDtypeStruct((b, B_WIDTH, s), BF16),
        jax.ShapeDtypeStruct((b, s, 2 * d), BF16),
    )
    return pl.pallas_call(
        functools.partial(_inproj_kernel, splits=splits),
        grid=(b, s // tm),
        in_specs=[
            pl.BlockSpec((1, tm, d), lambda i, j: (i, j, 0)),
            pl.BlockSpec((1, 1, tm), lambda i, j: (i, 0, j)),
            const(gpre.shape), const(wfm.shape), const(wg.shape), const(bg.shape),
            const(qng.shape), const(kvng.shape), const(wuq.shape), const(wukv.shape),
            const(inva.shape), const(invb.shape),
        ],
        out_specs=(
            fm_out(A_WIDTH), fm_out(A_KV_WIDTH), fm_out(A_KV_WIDTH),
            fm_out(B_HEADS * B_QK_PAD),
            pl.BlockSpec((1, tm, B_HEADS * B_QK_PAD), lambda i, j: (i, j, 0)),
            fm_out(B_WIDTH),
            pl.BlockSpec((1, tm, 2 * d), lambda i, j: (i, j, 0)),
        ),
        out_shape=out_shape,
        compiler_params=pltpu.CompilerParams(
            dimension_semantics=("arbitrary", "arbitrary"), vmem_limit_bytes=VMEM_LIMIT_BYTES),
        name="inproj",
    )(x, pos, gpre, wfm, wg, bg, qng, kvng, wuq, wukv, inva, invb)


def _swa_kernel(q_ref, k_ref, v_ref, sink_ref, o_ref):
    seq = q_ref.shape[2]
    cols = A_GROUP * BLOCK
    ones = jnp.ones((ONES_ROWS, BLOCK), BF16)

    def blocks(starts, first):
        win = BLOCK if first else 2 * BLOCK
        kj = lax.broadcasted_iota(jnp.int32, (BLOCK, cols), 0)
        qi = lax.broadcasted_iota(jnp.int32, (BLOCK, cols), 1) & (BLOCK - 1)
        lower = kj <= qi
        chains = [(qs, kvh) for qs in starts for kvh in range(A_KV_HEADS)]

        def key_start(qs):
            if first or isinstance(qs, int):
                return 0 if first else qs - BLOCK
            return pl.multiple_of(qs - BLOCK, BLOCK)

        scores = []
        for qs, kvh in chains:
            r = kvh * A_HEAD_DIM
            k_t = k_ref[0, r:r + A_HEAD_DIM, pl.ds(key_start(qs), win)]
            q_t = jnp.concatenate(
                [q_ref[0, (kvh * A_GROUP + g) * A_HEAD_DIM:(kvh * A_GROUP + g + 1) * A_HEAD_DIM,
                       pl.ds(qs, BLOCK)] for g in range(A_GROUP)], axis=1)
            scores.append(lax.dot_general(k_t, q_t, _TN, preferred_element_type=F32))

        probs = []
        for (qs, kvh), s_t in zip(chains, scores):
            if first:
                merged = jnp.where(lower, s_t, -jnp.inf)
            else:
                merged = jnp.where(lower, s_t[BLOCK:], s_t[:BLOCK])
            sink = sink_ref[kvh:kvh + 1, :] * LOG2_E
            m = jnp.maximum(jnp.max(merged, axis=0, keepdims=True), sink)
            e = jnp.exp2(merged - m)
            p_t = jnp.where(lower, e, 0.0).astype(BF16)
            if not first:
                p_t = jnp.concatenate([jnp.where(lower, 0.0, e).astype(BF16), p_t], axis=0)
            probs.append((p_t, jnp.exp2(sink - m)))

        outs = []
        for (qs, kvh), (p_t, sink_e) in zip(chains, probs):
            r = kvh * A_HEAD_DIM
            v_t = v_ref[0, r:r + A_HEAD_DIM, pl.ds(key_start(qs), win)]
            v_aug = jnp.concatenate([v_t] + [jnp.concatenate([ones] * (win // BLOCK), axis=1)],
                                    axis=0)
            o_aug = _dot(v_aug, p_t)
            denom = o_aug[A_HEAD_DIM:A_HEAD_DIM + 1] + sink_e
            outs.append(o_aug[:A_HEAD_DIM] * (1.0 / denom))

        for (qs, kvh), o_t in zip(chains, outs):
            for j in range(A_GROUP // 2):
                pair = jnp.concatenate(
                    [o_t[:, (2 * j) * BLOCK:(2 * j + 1) * BLOCK],
                     o_t[:, (2 * j + 1) * BLOCK:(2 * j + 2) * BLOCK]], axis=0)
                c = (kvh * (A_GROUP // 2) + j) * 2 * A_HEAD_DIM
                o_ref[0, pl.ds(qs, BLOCK), c:c + 2 * A_HEAD_DIM] = pair.T.astype(BF16)

    blocks([0], True)
    blocks([u * BLOCK for u in range(1, SWA_UNROLL)], False)

    def body(i, carry):
        qs = pl.multiple_of(i * (SWA_UNROLL * BLOCK), SWA_UNROLL * BLOCK)
        blocks([pl.multiple_of(qs + u * BLOCK, BLOCK) for u in range(SWA_UNROLL)], False)
        return carry

    assert (seq // BLOCK) % SWA_UNROLL == 0
    lax.fori_loop(1, seq // (SWA_UNROLL * BLOCK), body, 0)


def _swa(qa_t, ka_t, va_t, sink_rows):
    b, _, s = qa_t.shape
    return pl.pallas_call(
        _swa_kernel,
        grid=(b,),
        in_specs=[
            pl.BlockSpec((1, A_WIDTH, s), lambda i: (i, 0, 0)),
            pl.BlockSpec((1, A_KV_WIDTH, s), lambda i: (i, 0, 0)),
            pl.BlockSpec((1, A_KV_WIDTH, s), lambda i: (i, 0, 0)),
            pl.BlockSpec(sink_rows.shape, lambda i: (0, 0)),
        ],
        out_specs=pl.BlockSpec((1, s, A_WIDTH), lambda i: (i, 0, 0)),
        out_shape=jax.ShapeDtypeStruct((b, s, A_WIDTH), BF16),
        compiler_params=pltpu.CompilerParams(
            dimension_semantics=("arbitrary",), vmem_limit_bytes=VMEM_LIMIT_BYTES),
        name="swa",
    )(qa_t, ka_t, va_t, sink_rows)


def _mla_kernel(q_ref, k_ref, v_ref, o_ref, sa_ref, sb_ref, m_ref, acc_ref, *, heads, tq, tk):
    seq = q_ref.shape[2]
    n_q = seq // tq
    ones = jnp.ones((ONES_ROWS, tk), BF16)

    def scores_to(buf, q0, k0, qoff, qw):
        for hh in range(heads):
            r = hh * B_QK_PAD
            k_rows = k_ref[0, pl.ds(k0, tk), r:r + B_QK_PAD]
            q_t = q_ref[0, r:r + B_QK_PAD, pl.ds(q0 + qoff, qw)]
            buf[hh, :, qoff:qoff + qw] = _dot(k_rows, q_t)

    def consume(buf, k0, qoff, qw, mask=None):
        for hh in range(heads):
            s_t = buf[hh, :, qoff:qoff + qw]
            if mask is not None:
                s_t = jnp.where(mask, s_t, -jnp.inf)
            m_prev = m_ref[hh, :, qoff:qoff + qw]
            m_new = jnp.maximum(m_prev, jnp.max(s_t, axis=0, keepdims=True))
            alpha = jnp.exp2(m_prev - m_new)
            p = jnp.exp2(s_t - m_new).astype(BF16)
            v_t = jnp.concatenate(
                [v_ref[0, hh * V_DIM:(hh + 1) * V_DIM, pl.ds(k0, tk)], ones], axis=0)
            acc_ref[hh, :, qoff:qoff + qw] = alpha * acc_ref[hh, :, qoff:qoff + qw] + _dot(v_t, p)
            m_ref[hh, :, qoff:qoff + qw] = m_new

    def causal(width):
        kj = lax.broadcasted_iota(jnp.int32, (tk, width), 0)
        qi = lax.broadcasted_iota(jnp.int32, (tk, width), 1)
        return kj <= qi

    def q_tile(qt, carry):
        q0 = pl.multiple_of(qt * tq, tq)
        m_ref[...] = jnp.full(m_ref.shape, -jnp.inf, F32)
        acc_ref[...] = jnp.zeros(acc_ref.shape, F32)

        def pair(j):
            k0 = pl.multiple_of(j * tq, tq)
            scores_to(sb_ref, q0, k0 + tk, 0, tq)
            consume(sa_ref, k0, 0, tq)
            scores_to(sa_ref, q0, k0 + 2 * tk, 0, tq)
            consume(sb_ref, k0 + tk, 0, tq)

        for bit in reversed(range((n_q - 1).bit_length())):
            @pl.when((qt >> bit) & 1 == 1)
            def _(bit=bit):
                first = (qt >> (bit + 1)) << (bit + 1)
                for w in range(1 << bit):
                    pair(first + w)

        scores_to(sb_ref, q0, q0 + tk, tk, tq - tk)
        consume(sa_ref, q0, 0, tq, mask=causal(tq))
        q_next = pl.multiple_of(jnp.minimum(qt + 1, n_q - 1) * tq, tq)
        scores_to(sa_ref, q_next, 0, 0, tq)
        consume(sb_ref, q0 + tk, tk, tq - tk, mask=causal(tq - tk))

        outs = [acc_ref[hh, :V_DIM, :] * (1.0 / acc_ref[hh, V_DIM:V_DIM + 1, :])
                for hh in range(heads)]
        for hp in range(heads // 2):
            pair_t = jnp.concatenate(outs[2 * hp:2 * hp + 2], axis=0)
            o_ref[0, pl.ds(q0, tq), hp * 2 * V_DIM:(hp + 1) * 2 * V_DIM] = pair_t.T.astype(BF16)
        return carry

    scores_to(sa_ref, 0, 0, 0, tq)
    lax.fori_loop(0, n_q, q_tile, 0)


def _mla(qb_t, kb_t, vb_t, *, heads, tq, tk):
    assert tq == 2 * tk and heads % 2 == 0 and B_HEADS % heads == 0
    b, _, s = qb_t.shape
    return pl.pallas_call(
        functools.partial(_mla_kernel, heads=heads, tq=tq, tk=tk),
        grid=(b, B_HEADS // heads),
        in_specs=[
            pl.BlockSpec((1, heads * B_QK_PAD, s), lambda i, j: (i, j, 0)),
            pl.BlockSpec((1, s, heads * B_QK_PAD), lambda i, j: (i, 0, j)),
            pl.BlockSpec((1, heads * V_DIM, s), lambda i, j: (i, j, 0)),
        ],
        out_specs=pl.BlockSpec((1, s, heads * V_DIM), lambda i, j: (i, 0, j)),
        out_shape=jax.ShapeDtypeStruct((b, s, B_WIDTH), BF16),
        scratch_shapes=[
            pltpu.VMEM((heads, tk, tq), F32), pltpu.VMEM((heads, tk, tq), F32),
            pltpu.VMEM((heads, 1, tq), F32),
            pltpu.VMEM((heads, V_DIM + ONES_ROWS, tq), F32),
        ],
        compiler_params=pltpu.CompilerParams(
            dimension_semantics=("arbitrary", "arbitrary"), vmem_limit_bytes=VMEM_LIMIT_BYTES),
        name="mla",
    )(qb_t, kb_t, vb_t)


def _gelu_tanh(t):
    k1 = -2.0 * math.sqrt(2.0 / math.pi) * LOG2_E
    k3 = k1 * 0.044715
    return t * (1.0 / (1.0 + jnp.exp2(t * (k3 * (t * t) + k1))))


def _post_kernel(x_ref, ya_ref, yb_ref, g_ref, p_ref, wa_ref, wb_ref, wout_ref, gpost_ref,
                 gmlp_ref, wup_ref, cw_ref, cb_ref, wdown_ref, gmlppost_ref, gple_ref,
                 wpleg_ref, wple_ref, o_ref, ubuf_ref, *, tm):
    carry_rows = 8

    @pl.when(pl.program_id(1) == 0)
    def _():
        ubuf_ref[0:carry_rows, :] = jnp.zeros((carry_rows, 2 * D_FF), F32)

    x = x_ref[0]
    g = g_ref[0]
    ma = _dot(ya_ref[0], wa_ref[...])
    mb = _dot(yb_ref[0], wb_ref[...])
    mixed = g[:, :D_MODEL].astype(F32) * ma + g[:, D_MODEL:].astype(F32) * mb
    x1 = x + _rms_rows(_dot(mixed.astype(BF16), wout_ref[...]), gpost_ref[...])

    h2 = _rms_rows(x1, gmlp_ref[...]).astype(BF16)
    u = _dot(h2, wup_ref[...])
    ubuf_ref[carry_rows:carry_rows + tm, :] = u
    u1 = ubuf_ref[carry_rows - 1:carry_rows - 1 + tm, :]
    u2 = ubuf_ref[carry_rows - 2:carry_rows - 2 + tm, :]
    cw = cw_ref[...]
    y = cw[0:1] * u2 + cw[1:2] * u1 + cw[2:3] * u + cb_ref[...]
    ubuf_ref[0:carry_rows, :] = ubuf_ref[tm:tm + carry_rows, :]
    act = _gelu_tanh(y[:, :D_FF]) * y[:, D_FF:]
    ff = _dot(act.astype(BF16), wdown_ref[...])
    x2 = x1 + _rms_rows(ff, gmlppost_ref[...])

    e = _dot(p_ref[0].astype(BF16), wple_ref[...])
    gate = _sigmoid(_dot(_rms_rows(x2, gple_ref[...]).astype(BF16), wpleg_ref[...]))
    o_ref[0] = x2 + gate * e


def _post(x, ya, yb, gates, p, wa, wb, wout, gpost, gmlp, wup, cw, cb, wdown, gmlppost, gple,
          wpleg, wple, *, tm):
    b, s, d = x.shape
    const = lambda a: pl.BlockSpec(a.shape, lambda i, j: (0,) * a.ndim,
                                   pipeline_mode=pl.Buffered(1))
    rows = lambda n: pl.BlockSpec((1, tm, n), lambda i, j: (i, j, 0))
    weights = (wa, wb, wout, gpost, gmlp, wup, cw, cb, wdown, gmlppost, gple, wpleg, wple)
    return pl.pallas_call(
        functools.partial(_post_kernel, tm=tm),
        grid=(b, s // tm),
        in_specs=[rows(d), rows(A_WIDTH), rows(B_WIDTH), rows(2 * d), rows(PLE_DIM)]
        + [const(w) for w in weights],
        out_specs=rows(d),
        out_shape=jax.ShapeDtypeStruct((b, s, d), F32),
        scratch_shapes=[pltpu.VMEM((tm + 8, 2 * D_FF), F32)],
        compiler_params=pltpu.CompilerParams(
            dimension_semantics=("arbitrary", "arbitrary"), vmem_limit_bytes=VMEM_LIMIT_BYTES),
        name="post",
    )(x, ya, yb, gates, p, *weights)


def _layer(x, p_i, pos_f, inva, invb, attn_pre_norm, attn_post_norm, w_in, b_gate, sinks,
           q_a_norm, w_uq, kv_a_norm, w_ukv, w_branch_a, w_branch_b, w_out, mlp_pre_norm,
           mlp_post_norm, w_up, conv_w, conv_b, w_down, ple_norm, w_ple_gate, w_ple):
    row = lambda v: v.reshape(1, -1)
    col = lambda v: v.reshape(-1, 1)
    wfm = w_in[:, :N_FM].T.astype(BF16)
    wg = w_in[:, N_FM:].astype(BF16)
    wuq = jnp.pad(w_uq.reshape(Q_LORA, B_HEADS, NOPE_DIM + ROPE_DIM),
                  ((0, 0), (0, 0), (0, B_QK_PAD - NOPE_DIM - ROPE_DIM)))
    wuq = wuq.reshape(Q_LORA, B_HEADS * B_QK_PAD).T.astype(BF16)
    wukv = w_ukv.reshape(KV_LORA, B_HEADS, NOPE_DIM + V_DIM)
    wukv = jnp.concatenate([wukv[:, :, :NOPE_DIM].reshape(KV_LORA, -1),
                            wukv[:, :, NOPE_DIM:].reshape(KV_LORA, -1)], axis=1)
    wukv = wukv.T.astype(BF16)

    qa_t, ka_t, va_t, qb_t, kb_t, vb_t, gates = _inproj(
        x, pos_f, row(attn_pre_norm), wfm, wg, row(b_gate), col(q_a_norm), col(kv_a_norm),
        wuq, wukv, inva, invb, tm=1024, splits=4)

    sink_rows = jnp.repeat(sinks.astype(F32), BLOCK).reshape(A_KV_HEADS, A_GROUP * BLOCK)
    ya = _swa(qa_t, ka_t, va_t, sink_rows)
    yb = _mla(qb_t, kb_t, vb_t, heads=4, tq=512, tk=256)

    return _post(
        x, ya, yb, gates, p_i, w_branch_a.astype(BF16), w_branch_b.astype(BF16),
        w_out.astype(BF16), row(attn_post_norm), row(mlp_pre_norm), w_up.astype(BF16),
        conv_w, row(conv_b), w_down.astype(BF16), row(mlp_post_norm), row(ple_norm),
        w_ple_gate.astype(BF16), w_ple.astype(BF16), tm=256)


def kernel(x, p, positions, attn_pre_norm, attn_post_norm, w_in, b_gate, sinks, q_a_norm, w_uq,
           kv_a_norm, w_ukv, w_branch_a, w_branch_b, w_out, mlp_pre_norm, mlp_post_norm, w_up,
           conv_w, conv_b, w_down, ple_norm, w_ple_gate, w_ple):
    b, s, _ = x.shape
    pos_f = positions.astype(F32).reshape(b, 1, s)
    inva = (ROPE_THETA ** (-(jnp.arange(0, A_HEAD_DIM, 2, dtype=F32) / A_HEAD_DIM))).reshape(-1, 1)
    invb = (ROPE_THETA ** (-(jnp.arange(0, ROPE_DIM, 2, dtype=F32) / ROPE_DIM))).reshape(-1, 1)
    params = (attn_pre_norm, attn_post_norm, w_in, b_gate, sinks, q_a_norm, w_uq, kv_a_norm,
              w_ukv, w_branch_a, w_branch_b, w_out, mlp_pre_norm, mlp_post_norm, w_up, conv_w,
              conv_b, w_down, ple_norm, w_ple_gate, w_ple)
    for i in range(p.shape[0]):
        x = _layer(x, p[i], pos_f, inva, invb, *(w[i] for w in params))
    return x
```

```python
import functools
import math

import jax
import jax.numpy as jnp
from jax import lax
from jax.experimental import pallas as pl
from jax.experimental.pallas import tpu as pltpu

D_MODEL = 1024
PLE_DIM = 256
ROPE_THETA = 10000.0
RMS_EPS = 1e-6
BLOCK = 128
A_HEADS = 8
A_KV_HEADS = 2
A_GROUP = A_HEADS // A_KV_HEADS
A_HEAD_DIM = 64
A_WIDTH = A_HEADS * A_HEAD_DIM
A_KV_WIDTH = A_KV_HEADS * A_HEAD_DIM
B_HEADS = 8
Q_LORA = 256
KV_LORA = 128
NOPE_DIM = 64
ROPE_DIM = 32
V_DIM = 64
B_WIDTH = B_HEADS * V_DIM
B_QK_PAD = 128
ONES_ROWS = 16
SWA_UNROLL = 16
D_FF = 2816
CONV_W = 3
N_FM = A_WIDTH + 2 * A_KV_WIDTH + Q_LORA + KV_LORA + ROPE_DIM
VMEM_LIMIT_BYTES = 58 * 1024 * 1024
LOG2_E = math.log2(math.e)

BF16 = jnp.bfloat16
F32 = jnp.float32
_TN = (((0,), (0,)), ((), ()))
_NT = (((1,), (1,)), ((), ()))


def _dot(a, b):
    return jnp.dot(a, b, preferred_element_type=F32)


def _rms_rows(t, g):
    return t * lax.rsqrt(jnp.mean(t * t, axis=-1, keepdims=True) + RMS_EPS) * g


def _rms_cols(t, g):
    return t * lax.rsqrt(jnp.mean(t * t, axis=0, keepdims=True) + RMS_EPS) * g


def _sigmoid(t):
    return 1.0 / (1.0 + jnp.exp(-t))


def _rope_rows(t1, t2, c, s):
    return t1 * c - t2 * s, t2 * c + t1 * s


def _inproj_kernel(x_ref, pos_ref, gpre_ref, wfm_ref, wg_ref, bg_ref, qng_ref, kvng_ref,
                   wuq_ref, wukv_ref, inva_ref, invb_ref,
                   qa_ref, ka_ref, va_ref, qb_ref, kb_ref, vb_ref, gates_ref, *, splits):
    tm = x_ref.shape[1]
    tw = tm // splits
    parts = [slice(i * tw, (i + 1) * tw) for i in range(splits)]
    half_a = A_HEAD_DIM // 2
    half_b = ROPE_DIM // 2
    scale_a = LOG2_E * A_HEAD_DIM ** -0.5
    scale_b = LOG2_E * (NOPE_DIM + ROPE_DIM) ** -0.5
    zeros_pad = jnp.zeros((B_QK_PAD - NOPE_DIM - ROPE_DIM, tw), BF16)

    hs = [_rms_rows(x_ref[0, t, :], gpre_ref[...]).astype(BF16) for t in parts]
    fms = [lax.dot_general(wfm_ref[...], h, _NT, preferred_element_type=F32) for h in hs]

    for t, h in zip(parts, hs):
        gates = _dot(h, wg_ref[...]) + bg_ref[...]
        gates_ref[0, t, :] = _sigmoid(gates).astype(BF16)

    trig = []
    for t in parts:
        pos = pos_ref[0, :, t]
        ang_a = pos * inva_ref[...]
        ang_b = pos * invb_ref[...]
        trig.append((jnp.cos(ang_a), jnp.sin(ang_a), jnp.cos(ang_b), jnp.sin(ang_b)))

    for t, fm, (ca, sa, _, _) in zip(parts, fms, trig):
        for hd in range(A_HEADS):
            r = hd * A_HEAD_DIM
            o1, o2 = _rope_rows(fm[r:r + half_a], fm[r + half_a:r + A_HEAD_DIM], ca, sa)
            qa_ref[0, r:r + half_a, t] = (o1 * scale_a).astype(BF16)
            qa_ref[0, r + half_a:r + A_HEAD_DIM, t] = (o2 * scale_a).astype(BF16)
        for hd in range(A_KV_HEADS):
            r = hd * A_HEAD_DIM
            f = A_WIDTH + r
            o1, o2 = _rope_rows(fm[f:f + half_a], fm[f + half_a:f + A_HEAD_DIM], ca, sa)
            ka_ref[0, r:r + half_a, t] = o1.astype(BF16)
            ka_ref[0, r + half_a:r + A_HEAD_DIM, t] = o2.astype(BF16)
        f = A_WIDTH + A_KV_WIDTH
        va_ref[0, :, t] = fm[f:f + A_KV_WIDTH].astype(BF16)

    f_cq = A_WIDTH + 2 * A_KV_WIDTH
    f_ckv = f_cq + Q_LORA
    f_kr = f_ckv + KV_LORA
    qbs = [_dot(wuq_ref[...], _rms_cols(fm[f_cq:f_ckv], qng_ref[...]).astype(BF16)) * scale_b
           for fm in fms]
    kvs = [_dot(wukv_ref[...], _rms_cols(fm[f_ckv:f_kr], kvng_ref[...]).astype(BF16))
           for fm in fms]

    for t, fm, qb, kv, (_, _, cb, sb) in zip(parts, fms, qbs, kvs, trig):
        kr = fm[f_kr:f_kr + ROPE_DIM]
        k1, k2 = _rope_rows(kr[:half_b], kr[half_b:], cb, sb)
        k_tail = jnp.concatenate(
            [k1, k2, jnp.zeros((B_QK_PAD - NOPE_DIM - ROPE_DIM, tw), F32)], axis=0)
        for hd in range(B_HEADS):
            r = hd * B_QK_PAD
            p1 = r + NOPE_DIM
            qb_ref[0, r:p1, t] = qb[r:p1].astype(BF16)
            o1, o2 = _rope_rows(qb[p1:p1 + half_b], qb[p1 + half_b:p1 + ROPE_DIM], cb, sb)
            qb_ref[0, p1:p1 + half_b, t] = o1.astype(BF16)
            qb_ref[0, p1 + half_b:p1 + ROPE_DIM, t] = o2.astype(BF16)
            qb_ref[0, p1 + ROPE_DIM:r + B_QK_PAD, t] = zeros_pad
            k_head = jnp.concatenate([kv[hd * NOPE_DIM:(hd + 1) * NOPE_DIM], k_tail], axis=0)
            kb_ref[0, t, r:r + B_QK_PAD] = k_head.T.astype(BF16)
        vb_ref[0, :, t] = kv[B_HEADS * NOPE_DIM:].astype(BF16)


def _inproj(x, pos, gpre, wfm, wg, bg, qng, kvng, wuq, wukv, inva, invb, *, tm, splits):
    b, s, d = x.shape
    const = lambda shape: pl.BlockSpec(shape, lambda i, j: (0,) * len(shape))
    fm_out = lambda rows: pl.BlockSpec((1, rows, tm), lambda i, j: (i, 0, j))
    out_shape = (
        jax.ShapeDtypeStruct((b, A_WIDTH, s), BF16),
        jax.ShapeDtypeStruct((b, A_KV_WIDTH, s), BF16),
        jax.ShapeDtypeStruct((b, A_KV_WIDTH, s), BF16),
        jax.ShapeDtypeStruct((b, B_HEADS * B_QK_PAD, s), BF16),
        jax.ShapeDtypeStruct((b, s, B_HEADS * B_QK_PAD), BF16),
        jax.ShapeDtypeStruct((b, B_WIDTH, s), BF16),
        jax.ShapeDtypeStruct((b, s, 2 * d), BF16),
    )
    return pl.pallas_call(
        functools.partial(_inproj_kernel, splits=splits),
        grid=(b, s // tm),
        in_specs=[
            pl.BlockSpec((1, tm, d), lambda i, j: (i, j, 0)),
            pl.BlockSpec((1, 1, tm), lambda i, j: (i, 0, j)),
            const(gpre.shape), const(wfm.shape), const(wg.shape), const(bg.shape),
            const(qng.shape), const(kvng.shape), const(wuq.shape), const(wukv.shape),
            const(inva.shape), const(invb.shape),
        ],
        out_specs=(
            fm_out(A_WIDTH), fm_out(A_KV_WIDTH), fm_out(A_KV_WIDTH),
            fm_out(B_HEADS * B_QK_PAD),
            pl.BlockSpec((1, tm, B_HEADS * B_QK_PAD), lambda i, j: (i, j, 0)),
            fm_out(B_WIDTH),
            pl.BlockSpec((1, tm, 2 * d), lambda i, j: (i, j, 0)),
        ),
        out_shape=out_shape,
        compiler_params=pltpu.CompilerParams(
            dimension_semantics=("arbitrary", "arbitrary"), vmem_limit_bytes=VMEM_LIMIT_BYTES),
        name="inproj",
    )(x, pos, gpre, wfm, wg, bg, qng, kvng, wuq, wukv, inva, invb)


def _swa_kernel(q_ref, k_ref, v_ref, sink_ref, o_ref):
    seq = q_ref.shape[2]
    cols = A_GROUP * BLOCK
    ones = jnp.ones((ONES_ROWS, BLOCK), BF16)

    def blocks(starts, first):
        win = BLOCK if first else 2 * BLOCK
        kj = lax.broadcasted_iota(jnp.int32, (BLOCK, cols), 0)
        qi = lax.broadcasted_iota(jnp.int32, (BLOCK, cols), 1) & (BLOCK - 1)
        lower = kj <= qi
        chains = [(qs, kvh) for qs in starts for kvh in range(A_KV_HEADS)]

        def key_start(qs):
            if first or isinstance(qs, int):
                return 0 if first else qs - BLOCK
            return pl.multiple_of(qs - BLOCK, BLOCK)

        scores = []
        for qs, kvh in chains:
            r = kvh * A_HEAD_DIM
            k_t = k_ref[0, r:r + A_HEAD_DIM, pl.ds(key_start(qs), win)]
            q_t = jnp.concatenate(
                [q_ref[0, (kvh * A_GROUP + g) * A_HEAD_DIM:(kvh * A_GROUP + g + 1) * A_HEAD_DIM,
                       pl.ds(qs, BLOCK)] for g in range(A_GROUP)], axis=1)
            scores.append(lax.dot_general(k_t, q_t, _TN, preferred_element_type=F32))

        probs = []
        for (qs, kvh), s_t in zip(chains, scores):
            if first:
                merged = jnp.where(lower, s_t, -jnp.inf)
            else:
                merged = jnp.where(lower, s_t[BLOCK:], s_t[:BLOCK])
            sink = sink_ref[kvh:kvh + 1, :] * LOG2_E
            m = jnp.maximum(jnp.max(merged, axis=0, keepdims=True), sink)
            e = jnp.exp2(merged - m)
            p_t = jnp.where(lower, e, 0.0).astype(BF16)
            if not first:
                p_t = jnp.concatenate([jnp.where(lower, 0.0, e).astype(BF16), p_t], axis=0)
            probs.append((p_t, jnp.exp2(sink - m)))

        outs = []
        for (qs, kvh), (p_t, sink_e) in zip(chains, probs):
            r = kvh * A_HEAD_DIM
            v_t = v_ref[0, r:r + A_HEAD_DIM, pl.ds(key_start(qs), win)]
            v_aug = jnp.concatenate([v_t] + [jnp.concatenate([ones] * (win // BLOCK), axis=1)],
                                    axis=0)
            o_aug = _dot(v_aug, p_t)
            denom = o_aug[A_HEAD_DIM:A_HEAD_DIM + 1] + sink_e
            outs.append(o_aug[:A_HEAD_DIM] * (1.0 / denom))

        for (qs, kvh), o_t in zip(chains, outs):
            for j in range(A_GROUP // 2):
                pair = jnp.concatenate(
                    [o_t[:, (2 * j) * BLOCK:(2 * j + 1) * BLOCK],
                     o_t[:, (2 * j + 1) * BLOCK:(2 * j + 2) * BLOCK]], axis=0)
                c = (kvh * (A_GROUP // 2) + j) * 2 * A_HEAD_DIM
                o_ref[0, pl.ds(qs, BLOCK), c:c + 2 * A_HEAD_DIM] = pair.T.astype(BF16)

    blocks([0], True)
    blocks([u * BLOCK for u in range(1, SWA_UNROLL)], False)

    def body(i, carry):
        qs = pl.multiple_of(i * (SWA_UNROLL * BLOCK), SWA_UNROLL * BLOCK)
        blocks([pl.multiple_of(qs + u * BLOCK, BLOCK) for u in range(SWA_UNROLL)], False)
        return carry

    assert (seq // BLOCK) % SWA_UNROLL == 0
    lax.fori_loop(1, seq // (SWA_UNROLL * BLOCK), body, 0)


def _swa(qa_t, ka_t, va_t, sink_rows):
    b, _, s = qa_t.shape
    return pl.pallas_call(
        _swa_kernel,
        grid=(b,),
        in_specs=[
            pl.BlockSpec((1, A_WIDTH, s), lambda i: (i, 0, 0)),
            pl.BlockSpec((1, A_KV_WIDTH, s), lambda i: (i, 0, 0)),
            pl.BlockSpec((1, A_KV_WIDTH, s), lambda i: (i, 0, 0)),
            pl.BlockSpec(sink_rows.shape, lambda i: (0, 0)),
        ],
        out_specs=pl.BlockSpec((1, s, A_WIDTH), lambda i: (i, 0, 0)),
        out_shape=jax.ShapeDtypeStruct((b, s, A_WIDTH), BF16),
        compiler_params=pltpu.CompilerParams(
            dimension_semantics=("arbitrary",), vmem_limit_bytes=VMEM_LIMIT_BYTES),
        name="swa",
    )(qa_t, ka_t, va_t, sink_rows)


def _mla_kernel(q_ref, k_ref, v_ref, o_ref, sa_ref, sb_ref, m_ref, acc_ref, *, heads, tq, tk):
    seq = q_ref.shape[2]
    n_q = seq // tq
    ones = jnp.ones((ONES_ROWS, tk), BF16)

    def scores_to(buf, q0, k0, qoff, qw):
        for hh in range(heads):
            r = hh * B_QK_PAD
            k_rows = k_ref[0, pl.ds(k0, tk), r:r + B_QK_PAD]
            q_t = q_ref[0, r:r + B_QK_PAD, pl.ds(q0 + qoff, qw)]
            buf[hh, :, qoff:qoff + qw] = _dot(k_rows, q_t)

    def consume(buf, k0, qoff, qw, mask=None):
        for hh in range(heads):
            s_t = buf[hh, :, qoff:qoff + qw]
            if mask is not None:
                s_t = jnp.where(mask, s_t, -jnp.inf)
            m_prev = m_ref[hh, :, qoff:qoff + qw]
            m_new = jnp.maximum(m_prev, jnp.max(s_t, axis=0, keepdims=True))
            alpha = jnp.exp2(m_prev - m_new)
            p = jnp.exp2(s_t - m_new).astype(BF16)
            v_t = jnp.concatenate(
                [v_ref[0, hh * V_DIM:(hh + 1) * V_DIM, pl.ds(k0, tk)], ones], axis=0)
            acc_ref[hh, :, qoff:qoff + qw] = alpha * acc_ref[hh, :, qoff:qoff + qw] + _dot(v_t, p)
            m_ref[hh, :, qoff:qoff + qw] = m_new

    def causal(width):
        kj = lax.broadcasted_iota(jnp.int32, (tk, width), 0)
        qi = lax.broadcasted_iota(jnp.int32, (tk, width), 1)
        return kj <= qi

    def q_tile(qt, carry):
        q0 = pl.multiple_of(qt * tq, tq)
        m_ref[...] = jnp.full(m_ref.shape, -jnp.inf, F32)
        acc_ref[...] = jnp.zeros(acc_ref.shape, F32)

        def pair(j):
            k0 = pl.multiple_of(j * tq, tq)
            scores_to(sb_ref, q0, k0 + tk, 0, tq)
            consume(sa_ref, k0, 0, tq)
            scores_to(sa_ref, q0, k0 + 2 * tk, 0, tq)
            consume(sb_ref, k0 + tk, 0, tq)

        for bit in reversed(range((n_q - 1).bit_length())):
            @pl.when((qt >> bit) & 1 == 1)
            def _(bit=bit):
                first = (qt >> (bit + 1)) << (bit + 1)
                for w in range(1 << bit):
                    pair(first + w)

        scores_to(sb_ref, q0, q0 + tk, tk, tq - tk)
        consume(sa_ref, q0, 0, tq, mask=causal(tq))
        q_next = pl.multiple_of(jnp.minimum(qt + 1, n_q - 1) * tq, tq)
        scores_to(sa_ref, q_next, 0, 0, tq)
        consume(sb_ref, q0 + tk, tk, tq - tk, mask=causal(tq - tk))

        outs = [acc_ref[hh, :V_DIM, :] * (1.0 / acc_ref[hh, V_DIM:V_DIM + 1, :])
                for hh in range(heads)]
        for hp in range(heads // 2):
            pair_t = jnp.concatenate(outs[2 * hp:2 * hp + 2], axis=0)
            o_ref[0, pl.ds(q0, tq), hp * 2 * V_DIM:(hp + 1) * 2 * V_DIM] = pair_t.T.astype(BF16)
        return carry

    scores_to(sa_ref, 0, 0, 0, tq)
    lax.fori_loop(0, n_q, q_tile, 0)


def _mla(qb_t, kb_t, vb_t, *, heads, tq, tk):
    assert tq == 2 * tk and heads % 2 == 0 and B_HEADS % heads == 0
    b, _, s = qb_t.shape
    return pl.pallas_call(
        functools.partial(_mla_kernel, heads=heads, tq=tq, tk=tk),
        grid=(b, B_HEADS // heads),
        in_specs=[
            pl.BlockSpec((1, heads * B_QK_PAD, s), lambda i, j: (i, j, 0)),
            pl.BlockSpec((1, s, heads * B_QK_PAD), lambda i, j: (i, 0, j)),
            pl.BlockSpec((1, heads * V_DIM, s), lambda i, j: (i, j, 0)),
        ],
        out_specs=pl.BlockSpec((1, s, heads * V_DIM), lambda i, j: (i, 0, j)),
        out_shape=jax.ShapeDtypeStruct((b, s, B_WIDTH), BF16),
        scratch_shapes=[
            pltpu.VMEM((heads, tk, tq), F32), pltpu.VMEM((heads, tk, tq), F32),
            pltpu.VMEM((heads, 1, tq), F32),
            pltpu.VMEM((heads, V_DIM + ONES_ROWS, tq), F32),
        ],
        compiler_params=pltpu.CompilerParams(
            dimension_semantics=("arbitrary", "arbitrary"), vmem_limit_bytes=VMEM_LIMIT_BYTES),
        name="mla",
    )(qb_t, kb_t, vb_t)


def _gelu_tanh(t):
    k1 = -2.0 * math.sqrt(2.0 / math.pi) * LOG2_E
    k3 = k1 * 0.044715
    return t * (1.0 / (1.0 + jnp.exp2(t * (k3 * (t * t) + k1))))


def _post_kernel(x_ref, ya_ref, yb_ref, g_ref, p_ref, wa_ref, wb_ref, wout_ref, gpost_ref,
                 gmlp_ref, wup_ref, cw_ref, cb_ref, wdown_ref, gmlppost_ref, gple_ref,
                 wpleg_ref, wple_ref, o_ref, ubuf_ref, *, tm):
    carry_rows = 8

    @pl.when(pl.program_id(1) == 0)
    def _():
        ubuf_ref[0:carry_rows, :] = jnp.zeros((carry_rows, 2 * D_FF), F32)

    x = x_ref[0]
    g = g_ref[0]
    ma = _dot(ya_ref[0], wa_ref[...])
    mb = _dot(yb_ref[0], wb_ref[...])
    mixed = g[:, :D_MODEL].astype(F32) * ma + g[:, D_MODEL:].astype(F32) * mb
    x1 = x + _rms_rows(_dot(mixed.astype(BF16), wout_ref[...]), gpost_ref[...])

    h2 = _rms_rows(x1, gmlp_ref[...]).astype(BF16)
    u = _dot(h2, wup_ref[...])
    ubuf_ref[carry_rows:carry_rows + tm, :] = u
    u1 = ubuf_ref[carry_rows - 1:carry_rows - 1 + tm, :]
    u2 = ubuf_ref[carry_rows - 2:carry_rows - 2 + tm, :]
    cw = cw_ref[...]
    y = cw[0:1] * u2 + cw[1:2] * u1 + cw[2:3] * u + cb_ref[...]
    ubuf_ref[0:carry_rows, :] = ubuf_ref[tm:tm + carry_rows, :]
    act = _gelu_tanh(y[:, :D_FF]) * y[:, D_FF:]
    ff = _dot(act.astype(BF16), wdown_ref[...])
    x2 = x1 + _rms_rows(ff, gmlppost_ref[...])

    e = _dot(p_ref[0].astype(BF16), wple_ref[...])
    gate = _sigmoid(_dot(_rms_rows(x2, gple_ref[...]).astype(BF16), wpleg_ref[...]))
    o_ref[0] = x2 + gate * e


def _post(x, ya, yb, gates, p, wa, wb, wout, gpost, gmlp, wup, cw, cb, wdown, gmlppost, gple,
          wpleg, wple, *, tm):
    b, s, d = x.shape
    const = lambda a: pl.BlockSpec(a.shape, lambda i, j: (0,) * a.ndim,
                                   pipeline_mode=pl.Buffered(1))
    rows = lambda n: pl.BlockSpec((1, tm, n), lambda i, j: (i, j, 0))
    weights = (wa, wb, wout, gpost, gmlp, wup, cw, cb, wdown, gmlppost, gple, wpleg, wple)
    return pl.pallas_call(
        functools.partial(_post_kernel, tm=tm),
        grid=(b, s // tm),
        in_specs=[rows(d), rows(A_WIDTH), rows(B_WIDTH), rows(2 * d), rows(PLE_DIM)]
        + [const(w) for w in weights],
        out_specs=rows(d),
        out_shape=jax.ShapeDtypeStruct((b, s, d), F32),
        scratch_shapes=[pltpu.VMEM((tm + 8, 2 * D_FF), F32)],
        compiler_params=pltpu.CompilerParams(
            dimension_semantics=("arbitrary", "arbitrary"), vmem_limit_bytes=VMEM_LIMIT_BYTES),
        name="post",
    )(x, ya, yb, gates, p, *weights)


def _layer(x, p_i, pos_f, inva, invb, attn_pre_norm, attn_post_norm, w_in, b_gate, sinks,
           q_a_norm, w_uq, kv_a_norm, w_ukv, w_branch_a, w_branch_b, w_out, mlp_pre_norm,
           mlp_post_norm, w_up, conv_w, conv_b, w_down, ple_norm, w_ple_gate, w_ple):
    row = lambda v: v.reshape(1, -1)
    col = lambda v: v.reshape(-1, 1)
    w_in = w_in.astype(BF16)
    wfm = w_in[:, :N_FM].T
    wg = w_in[:, N_FM:]
    wuq = jnp.pad(w_uq.astype(BF16).reshape(Q_LORA, B_HEADS, NOPE_DIM + ROPE_DIM),
                  ((0, 0), (0, 0), (0, B_QK_PAD - NOPE_DIM - ROPE_DIM)))
    wuq = wuq.reshape(Q_LORA, B_HEADS * B_QK_PAD).T
    wukv = w_ukv.astype(BF16).reshape(KV_LORA, B_HEADS, NOPE_DIM + V_DIM)
    wukv = jnp.concatenate([wukv[:, :, :NOPE_DIM].reshape(KV_LORA, -1),
                            wukv[:, :, NOPE_DIM:].reshape(KV_LORA, -1)], axis=1)
    wukv = wukv.T

    qa_t, ka_t, va_t, qb_t, kb_t, vb_t, gates = _inproj(
        x, pos_f, row(attn_pre_norm), wfm, wg, row(b_gate), col(q_a_norm), col(kv_a_norm),
        wuq, wukv, inva, invb, tm=1024, splits=4)

    sink_rows = jnp.repeat(sinks.astype(F32), BLOCK).reshape(A_KV_HEADS, A_GROUP * BLOCK)
    ya = _swa(qa_t, ka_t, va_t, sink_rows)
    yb = _mla(qb_t, kb_t, vb_t, heads=4, tq=512, tk=256)

    return _post(
        x, ya, yb, gates, p_i, w_branch_a.astype(BF16), w_branch_b.astype(BF16),
        w_out.astype(BF16), row(attn_post_norm), row(mlp_pre_norm), w_up.astype(BF16),
        conv_w, row(conv_b), w_down.astype(BF16), row(mlp_post_norm), row(ple_norm),
        w_ple_gate.astype(BF16), w_ple.astype(BF16), tm=256)


def kernel(x, p, positions, attn_pre_norm, attn_post_norm, w_in, b_gate, sinks, q_a_norm, w_uq,
           kv_a_norm, w_ukv, w_branch_a, w_branch_b, w_out, mlp_pre_norm, mlp_post_norm, w_up,
           conv_w, conv_b, w_down, ple_norm, w_ple_gate, w_ple):
    b, s, _ = x.shape
    pos_f = positions.astype(F32).reshape(b, 1, s)
    inva = (ROPE_THETA ** (-(jnp.arange(0, A_HEAD_DIM, 2, dtype=F32) / A_HEAD_DIM))).reshape(-1, 1)
    invb = (ROPE_THETA ** (-(jnp.arange(0, ROPE_DIM, 2, dtype=F32) / ROPE_DIM))).reshape(-1, 1)
    params = (attn_pre_norm, attn_post_norm, w_in, b_gate, sinks, q_a_norm, w_uq, kv_a_norm,
              w_ukv, w_branch_a, w_branch_b, w_out, mlp_pre_norm, mlp_post_norm, w_up, conv_w,
              conv_b, w_down, ple_norm, w_ple_gate, w_ple)
    for i in range(p.shape[0]):
        x = _layer(x, p[i], pos_f, inva, invb, *(w[i] for w in params))
    return x
```

```python
import functools
import math

import jax
import jax.numpy as jnp
from jax import lax
from jax.experimental import pallas as pl
from jax.experimental.pallas import tpu as pltpu

D_MODEL = 1024
PLE_DIM = 256
ROPE_THETA = 10000.0
RMS_EPS = 1e-6
BLOCK = 128
A_HEADS = 8
A_KV_HEADS = 2
A_GROUP = A_HEADS // A_KV_HEADS
A_HEAD_DIM = 64
A_WIDTH = A_HEADS * A_HEAD_DIM
A_KV_WIDTH = A_KV_HEADS * A_HEAD_DIM
B_HEADS = 8
Q_LORA = 256
KV_LORA = 128
NOPE_DIM = 64
ROPE_DIM = 32
V_DIM = 64
B_WIDTH = B_HEADS * V_DIM
B_QK_PAD = 128
ONES_ROWS = 16
SWA_UNROLL = 16
D_FF = 2816
CONV_W = 3
N_FM = A_WIDTH + 2 * A_KV_WIDTH + Q_LORA + KV_LORA + ROPE_DIM
VMEM_LIMIT_BYTES = 58 * 1024 * 1024
LOG2_E = math.log2(math.e)

BF16 = jnp.bfloat16
F32 = jnp.float32
_TN = (((0,), (0,)), ((), ()))
_NT = (((1,), (1,)), ((), ()))


def _dot(a, b):
    return jnp.dot(a, b, preferred_element_type=F32)


def _rms_rows(t, g):
    return t * lax.rsqrt(jnp.mean(t * t, axis=-1, keepdims=True) + RMS_EPS) * g


def _rms_cols(t, g):
    return t * lax.rsqrt(jnp.mean(t * t, axis=0, keepdims=True) + RMS_EPS) * g


def _sigmoid(t):
    return 1.0 / (1.0 + jnp.exp(-t))


def _rope_rows(t1, t2, c, s):
    return t1 * c - t2 * s, t2 * c + t1 * s


def _inproj_kernel(x_ref, pos_ref, gpre_ref, wfm_ref, wg_ref, bg_ref, qng_ref, kvng_ref,
                   wuq_ref, wukv_ref, inva_ref, invb_ref,
                   qa_ref, ka_ref, va_ref, qb_ref, kb_ref, vb_ref, gates_ref, *, splits):
    tm = x_ref.shape[1]
    tw = tm // splits
    parts = [slice(i * tw, (i + 1) * tw) for i in range(splits)]
    half_a = A_HEAD_DIM // 2
    half_b = ROPE_DIM // 2
    scale_a = LOG2_E * A_HEAD_DIM ** -0.5
    scale_b = LOG2_E * (NOPE_DIM + ROPE_DIM) ** -0.5
    zeros_pad = jnp.zeros((B_QK_PAD - NOPE_DIM - ROPE_DIM, tw), BF16)

    hs = [_rms_rows(x_ref[0, t, :], gpre_ref[...]).astype(BF16) for t in parts]
    fms = [lax.dot_general(wfm_ref[...], h, _NT, preferred_element_type=F32) for h in hs]

    for t, h in zip(parts, hs):
        gates = _dot(h, wg_ref[...]) + bg_ref[...]
        gates_ref[0, t, :] = _sigmoid(gates).astype(BF16)

    trig = []
    for t in parts:
        pos = pos_ref[0, :, t]
        ang_a = pos * inva_ref[...]
        ang_b = pos * invb_ref[...]
        trig.append((jnp.cos(ang_a), jnp.sin(ang_a), jnp.cos(ang_b), jnp.sin(ang_b)))

    for t, fm, (ca, sa, _, _) in zip(parts, fms, trig):
        for hd in range(A_HEADS):
            r = hd * A_HEAD_DIM
            o1, o2 = _rope_rows(fm[r:r + half_a], fm[r + half_a:r + A_HEAD_DIM], ca, sa)
            qa_ref[0, r:r + half_a, t] = (o1 * scale_a).astype(BF16)
            qa_ref[0, r + half_a:r + A_HEAD_DIM, t] = (o2 * scale_a).astype(BF16)
        for hd in range(A_KV_HEADS):
            r = hd * A_HEAD_DIM
            f = A_WIDTH + r
            o1, o2 = _rope_rows(fm[f:f + half_a], fm[f + half_a:f + A_HEAD_DIM], ca, sa)
            ka_ref[0, r:r + half_a, t] = o1.astype(BF16)
            ka_ref[0, r + half_a:r + A_HEAD_DIM, t] = o2.astype(BF16)
        f = A_WIDTH + A_KV_WIDTH
        va_ref[0, :, t] = fm[f:f + A_KV_WIDTH].astype(BF16)

    f_cq = A_WIDTH + 2 * A_KV_WIDTH
    f_ckv = f_cq + Q_LORA
    f_kr = f_ckv + KV_LORA
    qbs = [_dot(wuq_ref[...], _rms_cols(fm[f_cq:f_ckv], qng_ref[...]).astype(BF16)) * scale_b
           for fm in fms]
    kvs = [_dot(wukv_ref[...], _rms_cols(fm[f_ckv:f_kr], kvng_ref[...]).astype(BF16))
           for fm in fms]

    for t, fm, qb, kv, (_, _, cb, sb) in zip(parts, fms, qbs, kvs, trig):
        kr = fm[f_kr:f_kr + ROPE_DIM]
        k1, k2 = _rope_rows(kr[:half_b], kr[half_b:], cb, sb)
        k_tail = jnp.concatenate(
            [k1, k2, jnp.zeros((B_QK_PAD - NOPE_DIM - ROPE_DIM, tw), F32)], axis=0)
        for hd in range(B_HEADS):
            r = hd * B_QK_PAD
            p1 = r + NOPE_DIM
            qb_ref[0, r:p1, t] = qb[r:p1].astype(BF16)
            o1, o2 = _rope_rows(qb[p1:p1 + half_b], qb[p1 + half_b:p1 + ROPE_DIM], cb, sb)
            qb_ref[0, p1:p1 + half_b, t] = o1.astype(BF16)
            qb_ref[0, p1 + half_b:p1 + ROPE_DIM, t] = o2.astype(BF16)
            qb_ref[0, p1 + ROPE_DIM:r + B_QK_PAD, t] = zeros_pad
            k_head = jnp.concatenate([kv[hd * NOPE_DIM:(hd + 1) * NOPE_DIM], k_tail], axis=0)
            kb_ref[0, t, r:r + B_QK_PAD] = k_head.T.astype(BF16)
        vb_ref[0, :, t] = kv[B_HEADS * NOPE_DIM:].astype(BF16)


def _inproj(x, pos, gpre, wfm, wg, bg, qng, kvng, wuq, wukv, inva, invb, *, tm, splits):
    b, s, d = x.shape
    const = lambda shape: pl.BlockSpec(shape, lambda i, j: (0,) * len(shape))
    fm_out = lambda rows: pl.BlockSpec((1, rows, tm), lambda i, j: (i, 0, j))
    out_shape = (
        jax.ShapeDtypeStruct((b, A_WIDTH, s), BF16),
        jax.ShapeDtypeStruct((b, A_KV_WIDTH, s), BF16),
        jax.ShapeDtypeStruct((b, A_KV_WIDTH, s), BF16),
        jax.ShapeDtypeStruct((b, B_HEADS * B_QK_PAD, s), BF16),
        jax.ShapeDtypeStruct((b, s, B_HEADS * B_QK_PAD), BF16),
        jax.ShapeDtypeStruct((b, B_WIDTH, s), BF16),
        jax.ShapeDtypeStruct((b, s, 2 * d), BF16),
    )
    return pl.pallas_call(
        functools.partial(_inproj_kernel, splits=splits),
        grid=(b, s // tm),
        in_specs=[
            pl.BlockSpec((1, tm, d), lambda i, j: (i, j, 0)),
            pl.BlockSpec((1, 1, tm), lambda i, j: (i, 0, j)),
            const(gpre.shape), const(wfm.shape), const(wg.shape), const(bg.shape),
            const(qng.shape), const(kvng.shape), const(wuq.shape), const(wukv.shape),
            const(inva.shape), const(invb.shape),
        ],
        out_specs=(
            fm_out(A_WIDTH), fm_out(A_KV_WIDTH), fm_out(A_KV_WIDTH),
            fm_out(B_HEADS * B_QK_PAD),
            pl.BlockSpec((1, tm, B_HEADS * B_QK_PAD), lambda i, j: (i, j, 0)),
            fm_out(B_WIDTH),
            pl.BlockSpec((1, tm, 2 * d), lambda i, j: (i, j, 0)),
        ),
        out_shape=out_shape,
        compiler_params=pltpu.CompilerParams(
            dimension_semantics=("arbitrary", "arbitrary"), vmem_limit_bytes=VMEM_LIMIT_BYTES),
        name="inproj",
    )(x, pos, gpre, wfm, wg, bg, qng, kvng, wuq, wukv, inva, invb)


def _swa_kernel(q_ref, k_ref, v_ref, sink_ref, o_ref):
    seq = q_ref.shape[2]
    cols = A_GROUP * BLOCK
    ones = jnp.ones((ONES_ROWS, BLOCK), BF16)

    def blocks(starts, first):
        win = BLOCK if first else 2 * BLOCK
        kj = lax.broadcasted_iota(jnp.int32, (BLOCK, cols), 0)
        qi = lax.broadcasted_iota(jnp.int32, (BLOCK, cols), 1) & (BLOCK - 1)
        lower = kj <= qi
        chains = [(qs, kvh) for qs in starts for kvh in range(A_KV_HEADS)]

        def key_start(qs):
            if first or isinstance(qs, int):
                return 0 if first else qs - BLOCK
            return pl.multiple_of(qs - BLOCK, BLOCK)

        scores = []
        for qs, kvh in chains:
            r = kvh * A_HEAD_DIM
            k_t = k_ref[0, r:r + A_HEAD_DIM, pl.ds(key_start(qs), win)]
            q_t = jnp.concatenate(
                [q_ref[0, (kvh * A_GROUP + g) * A_HEAD_DIM:(kvh * A_GROUP + g + 1) * A_HEAD_DIM,
                       pl.ds(qs, BLOCK)] for g in range(A_GROUP)], axis=1)
            scores.append(lax.dot_general(k_t, q_t, _TN, preferred_element_type=F32))

        probs = []
        for (qs, kvh), s_t in zip(chains, scores):
            if first:
                merged = jnp.where(lower, s_t, -jnp.inf)
            else:
                merged = jnp.where(lower, s_t[BLOCK:], s_t[:BLOCK])
            sink = sink_ref[kvh:kvh + 1, :] * LOG2_E
            m = jnp.maximum(jnp.max(merged, axis=0, keepdims=True), sink)
            e = jnp.exp2(merged - m)
            p_t = jnp.where(lower, e, 0.0).astype(BF16)
            if not first:
                p_t = jnp.concatenate([jnp.where(lower, 0.0, e).astype(BF16), p_t], axis=0)
            probs.append((p_t, jnp.exp2(sink - m)))

        outs = []
        for (qs, kvh), (p_t, sink_e) in zip(chains, probs):
            r = kvh * A_HEAD_DIM
            v_t = v_ref[0, r:r + A_HEAD_DIM, pl.ds(key_start(qs), win)]
            v_aug = jnp.concatenate([v_t] + [jnp.concatenate([ones] * (win // BLOCK), axis=1)],
                                    axis=0)
            o_aug = _dot(v_aug, p_t)
            denom = o_aug[A_HEAD_DIM:A_HEAD_DIM + 1] + sink_e
            outs.append(o_aug[:A_HEAD_DIM] * (1.0 / denom))

        for (qs, kvh), o_t in zip(chains, outs):
            for j in range(A_GROUP // 2):
                pair = jnp.concatenate(
                    [o_t[:, (2 * j) * BLOCK:(2 * j + 1) * BLOCK],
                     o_t[:, (2 * j + 1) * BLOCK:(2 * j + 2) * BLOCK]], axis=0)
                c = (kvh * (A_GROUP // 2) + j) * 2 * A_HEAD_DIM
                o_ref[0, pl.ds(qs, BLOCK), c:c + 2 * A_HEAD_DIM] = pair.T.astype(BF16)

    blocks([0], True)
    blocks([u * BLOCK for u in range(1, SWA_UNROLL)], False)

    def body(i, carry):
        qs = pl.multiple_of(i * (SWA_UNROLL * BLOCK), SWA_UNROLL * BLOCK)
        blocks([pl.multiple_of(qs + u * BLOCK, BLOCK) for u in range(SWA_UNROLL)], False)
        return carry

    assert (seq // BLOCK) % SWA_UNROLL == 0
    lax.fori_loop(1, seq // (SWA_UNROLL * BLOCK), body, 0)


def _swa(qa_t, ka_t, va_t, sink_rows):
    b, _, s = qa_t.shape
    return pl.pallas_call(
        _swa_kernel,
        grid=(b,),
        in_specs=[
            pl.BlockSpec((1, A_WIDTH, s), lambda i: (i, 0, 0)),
            pl.BlockSpec((1, A_KV_WIDTH, s), lambda i: (i, 0, 0)),
            pl.BlockSpec((1, A_KV_WIDTH, s), lambda i: (i, 0, 0)),
            pl.BlockSpec(sink_rows.shape, lambda i: (0, 0)),
        ],
        out_specs=pl.BlockSpec((1, s, A_WIDTH), lambda i: (i, 0, 0)),
        out_shape=jax.ShapeDtypeStruct((b, s, A_WIDTH), BF16),
        compiler_params=pltpu.CompilerParams(
            dimension_semantics=("arbitrary",), vmem_limit_bytes=VMEM_LIMIT_BYTES),
        name="swa",
    )(qa_t, ka_t, va_t, sink_rows)


def _mla_kernel(q_ref, k_ref, v_ref, o_ref, sa_ref, sb_ref, xa_ref, xb_ref, m_ref, acc_ref, *,
                heads, tq, tk):
    seq = q_ref.shape[2]
    n_q = seq // tq
    ones = jnp.ones((ONES_ROWS, tk), BF16)
    buf_a = (sa_ref, xa_ref)
    buf_b = (sb_ref, xb_ref)

    def scores_to(buf, q0, k0, qoff, qw):
        s_ref, x_ref = buf
        for hh in range(heads):
            r = hh * B_QK_PAD
            k_rows = k_ref[0, pl.ds(k0, tk), r:r + B_QK_PAD]
            q_t = q_ref[0, r:r + B_QK_PAD, pl.ds(q0 + qoff, qw)]
            s_t = _dot(k_rows, q_t)
            s_ref[hh, :, qoff:qoff + qw] = s_t
            x_ref[hh, :, qoff:qoff + qw] = jnp.max(s_t, axis=0, keepdims=True)

    def consume(buf, k0, qoff, qw, mask=None):
        s_ref, x_ref = buf
        for hh in range(heads):
            s_t = s_ref[hh, :, qoff:qoff + qw]
            if mask is None:
                m_tile = x_ref[hh, :, qoff:qoff + qw]
            else:
                s_t = jnp.where(mask, s_t, -jnp.inf)
                m_tile = jnp.max(s_t, axis=0, keepdims=True)
            m_prev = m_ref[hh, :, qoff:qoff + qw]
            m_new = jnp.maximum(m_prev, m_tile)
            alpha = jnp.exp2(m_prev - m_new)
            p = jnp.exp2(s_t - m_new).astype(BF16)
            v_t = jnp.concatenate(
                [v_ref[0, hh * V_DIM:(hh + 1) * V_DIM, pl.ds(k0, tk)], ones], axis=0)
            acc_ref[hh, :, qoff:qoff + qw] = alpha * acc_ref[hh, :, qoff:qoff + qw] + _dot(v_t, p)
            m_ref[hh, :, qoff:qoff + qw] = m_new

    def causal(width):
        kj = lax.broadcasted_iota(jnp.int32, (tk, width), 0)
        qi = lax.broadcasted_iota(jnp.int32, (tk, width), 1)
        return kj <= qi

    def q_tile(qt, carry):
        q0 = pl.multiple_of(qt * tq, tq)
        m_ref[...] = jnp.full(m_ref.shape, -jnp.inf, F32)
        acc_ref[...] = jnp.zeros(acc_ref.shape, F32)

        def pair(j):
            k0 = pl.multiple_of(j * tq, tq)
            scores_to(buf_b, q0, k0 + tk, 0, tq)
            consume(buf_a, k0, 0, tq)
            scores_to(buf_a, q0, k0 + 2 * tk, 0, tq)
            consume(buf_b, k0 + tk, 0, tq)

        for bit in reversed(range((n_q - 1).bit_length())):
            @pl.when((qt >> bit) & 1 == 1)
            def _(bit=bit):
                first = (qt >> (bit + 1)) << (bit + 1)
                for w in range(1 << bit):
                    pair(first + w)

        scores_to(buf_b, q0, q0 + tk, tk, tq - tk)
        consume(buf_a, q0, 0, tq, mask=causal(tq))
        q_next = pl.multiple_of(jnp.minimum(qt + 1, n_q - 1) * tq, tq)
        scores_to(buf_a, q_next, 0, 0, tq)
        consume(buf_b, q0 + tk, tk, tq - tk, mask=causal(tq - tk))

        outs = [acc_ref[hh, :V_DIM, :] * (1.0 / acc_ref[hh, V_DIM:V_DIM + 1, :])
                for hh in range(heads)]
        for hp in range(heads // 2):
            pair_t = jnp.concatenate(outs[2 * hp:2 * hp + 2], axis=0)
            o_ref[0, pl.ds(q0, tq), hp * 2 * V_DIM:(hp + 1) * 2 * V_DIM] = pair_t.T.astype(BF16)
        return carry

    scores_to(buf_a, 0, 0, 0, tq)
    lax.fori_loop(0, n_q, q_tile, 0)


def _mla(qb_t, kb_t, vb_t, *, heads, tq, tk):
    assert tq == 2 * tk and heads % 2 == 0 and B_HEADS % heads == 0
    b, _, s = qb_t.shape
    return pl.pallas_call(
        functools.partial(_mla_kernel, heads=heads, tq=tq, tk=tk),
        grid=(b, B_HEADS // heads),
        in_specs=[
            pl.BlockSpec((1, heads * B_QK_PAD, s), lambda i, j: (i, j, 0)),
            pl.BlockSpec((1, s, heads * B_QK_PAD), lambda i, j: (i, 0, j)),
            pl.BlockSpec((1, heads * V_DIM, s), lambda i, j: (i, j, 0)),
        ],
        out_specs=pl.BlockSpec((1, s, heads * V_DIM), lambda i, j: (i, 0, j)),
        out_shape=jax.ShapeDtypeStruct((b, s, B_WIDTH), BF16),
        scratch_shapes=[
            pltpu.VMEM((heads, tk, tq), F32), pltpu.VMEM((heads, tk, tq), F32),
            pltpu.VMEM((heads, 1, tq), F32), pltpu.VMEM((heads, 1, tq), F32),
            pltpu.VMEM((heads, 1, tq), F32),
            pltpu.VMEM((heads, V_DIM + ONES_ROWS, tq), F32),
        ],
        compiler_params=pltpu.CompilerParams(
            dimension_semantics=("arbitrary", "arbitrary"), vmem_limit_bytes=VMEM_LIMIT_BYTES),
        name="mla",
    )(qb_t, kb_t, vb_t)


def _gelu_tanh(t):
    k1 = -2.0 * math.sqrt(2.0 / math.pi) * LOG2_E
    k3 = k1 * 0.044715
    return t * (1.0 / (1.0 + jnp.exp2(t * (k3 * (t * t) + k1))))


def _post_kernel(x_ref, ya_ref, yb_ref, g_ref, p_ref, wa_ref, wb_ref, wout_ref, gpost_ref,
                 gmlp_ref, wup_ref, cw_ref, cb_ref, wdown_ref, gmlppost_ref, gple_ref,
                 wpleg_ref, wple_ref, o_ref, ubuf_ref, *, tm):
    carry_rows = 8

    @pl.when(pl.program_id(1) == 0)
    def _():
        ubuf_ref[0:carry_rows, :] = jnp.zeros((carry_rows, 2 * D_FF), F32)

    x = x_ref[0]
    g = g_ref[0]
    ma = _dot(ya_ref[0], wa_ref[...])
    mb = _dot(yb_ref[0], wb_ref[...])
    mixed = g[:, :D_MODEL].astype(F32) * ma + g[:, D_MODEL:].astype(F32) * mb
    x1 = x + _rms_rows(_dot(mixed.astype(BF16), wout_ref[...]), gpost_ref[...])

    h2 = _rms_rows(x1, gmlp_ref[...]).astype(BF16)
    u = _dot(h2, wup_ref[...])
    ubuf_ref[carry_rows:carry_rows + tm, :] = u
    u1 = ubuf_ref[carry_rows - 1:carry_rows - 1 + tm, :]
    u2 = ubuf_ref[carry_rows - 2:carry_rows - 2 + tm, :]
    cw = cw_ref[...]
    y = cw[0:1] * u2 + cw[1:2] * u1 + cw[2:3] * u + cb_ref[...]
    ubuf_ref[0:carry_rows, :] = ubuf_ref[tm:tm + carry_rows, :]
    act = _gelu_tanh(y[:, :D_FF]) * y[:, D_FF:]
    ff = _dot(act.astype(BF16), wdown_ref[...])
    x2 = x1 + _rms_rows(ff, gmlppost_ref[...])

    e = _dot(p_ref[0].astype(BF16), wple_ref[...])
    gate = _sigmoid(_dot(_rms_rows(x2, gple_ref[...]).astype(BF16), wpleg_ref[...]))
    o_ref[0] = x2 + gate * e


def _post(x, ya, yb, gates, p, wa, wb, wout, gpost, gmlp, wup, cw, cb, wdown, gmlppost, gple,
          wpleg, wple, *, tm):
    b, s, d = x.shape
    const = lambda a: pl.BlockSpec(a.shape, lambda i, j: (0,) * a.ndim,
                                   pipeline_mode=pl.Buffered(1))
    rows = lambda n: pl.BlockSpec((1, tm, n), lambda i, j: (i, j, 0))
    weights = (wa, wb, wout, gpost, gmlp, wup, cw, cb, wdown, gmlppost, gple, wpleg, wple)
    return pl.pallas_call(
        functools.partial(_post_kernel, tm=tm),
        grid=(b, s // tm),
        in_specs=[rows(d), rows(A_WIDTH), rows(B_WIDTH), rows(2 * d), rows(PLE_DIM)]
        + [const(w) for w in weights],
        out_specs=rows(d),
        out_shape=jax.ShapeDtypeStruct((b, s, d), F32),
        scratch_shapes=[pltpu.VMEM((tm + 8, 2 * D_FF), F32)],
        compiler_params=pltpu.CompilerParams(
            dimension_semantics=("arbitrary", "arbitrary"), vmem_limit_bytes=VMEM_LIMIT_BYTES),
        name="post",
    )(x, ya, yb, gates, p, *weights)


def _layer(x, p_i, pos_f, inva, invb, attn_pre_norm, attn_post_norm, w_in, b_gate, sinks,
           q_a_norm, w_uq, kv_a_norm, w_ukv, w_branch_a, w_branch_b, w_out, mlp_pre_norm,
           mlp_post_norm, w_up, conv_w, conv_b, w_down, ple_norm, w_ple_gate, w_ple):
    row = lambda v: v.reshape(1, -1)
    col = lambda v: v.reshape(-1, 1)
    w_in = w_in.astype(BF16)
    wfm = w_in[:, :N_FM].T
    wg = w_in[:, N_FM:]
    wuq = jnp.pad(w_uq.astype(BF16).reshape(Q_LORA, B_HEADS, NOPE_DIM + ROPE_DIM),
                  ((0, 0), (0, 0), (0, B_QK_PAD - NOPE_DIM - ROPE_DIM)))
    wuq = wuq.reshape(Q_LORA, B_HEADS * B_QK_PAD).T
    wukv = w_ukv.astype(BF16).reshape(KV_LORA, B_HEADS, NOPE_DIM + V_DIM)
    wukv = jnp.concatenate([wukv[:, :, :NOPE_DIM].reshape(KV_LORA, -1),
                            wukv[:, :, NOPE_DIM:].reshape(KV_LORA, -1)], axis=1)
    wukv = wukv.T

    qa_t, ka_t, va_t, qb_t, kb_t, vb_t, gates = _inproj(
        x, pos_f, row(attn_pre_norm), wfm, wg, row(b_gate), col(q_a_norm), col(kv_a_norm),
        wuq, wukv, inva, invb, tm=1024, splits=4)

    sink_rows = jnp.repeat(sinks.astype(F32), BLOCK).reshape(A_KV_HEADS, A_GROUP * BLOCK)
    ya = _swa(qa_t, ka_t, va_t, sink_rows)
    yb = _mla(qb_t, kb_t, vb_t, heads=4, tq=512, tk=256)

    return _post(
        x, ya, yb, gates, p_i, w_branch_a.astype(BF16), w_branch_b.astype(BF16),
        w_out.astype(BF16), row(attn_post_norm), row(mlp_pre_norm), w_up.astype(BF16),
        conv_w, row(conv_b), w_down.astype(BF16), row(mlp_post_norm), row(ple_norm),
        w_ple_gate.astype(BF16), w_ple.astype(BF16), tm=256)


def kernel(x, p, positions, attn_pre_norm, attn_post_norm, w_in, b_gate, sinks, q_a_norm, w_uq,
           kv_a_norm, w_ukv, w_branch_a, w_branch_b, w_out, mlp_pre_norm, mlp_post_norm, w_up,
           conv_w, conv_b, w_down, ple_norm, w_ple_gate, w_ple):
    b, s, _ = x.shape
    pos_f = positions.astype(F32).reshape(b, 1, s)
    inva = (ROPE_THETA ** (-(jnp.arange(0, A_HEAD_DIM, 2, dtype=F32) / A_HEAD_DIM))).reshape(-1, 1)
    invb = (ROPE_THETA ** (-(jnp.arange(0, ROPE_DIM, 2, dtype=F32) / ROPE_DIM))).reshape(-1, 1)
    params = (attn_pre_norm, attn_post_norm, w_in, b_gate, sinks, q_a_norm, w_uq, kv_a_norm,
              w_ukv, w_branch_a, w_branch_b, w_out, mlp_pre_norm, mlp_post_norm, w_up, conv_w,
              conv_b, w_down, ple_norm, w_ple_gate, w_ple)
    for i in range(p.shape[0]):
        x = _layer(x, p[i], pos_f, inva, invb, *(w[i] for w in params))
    return x
```
